```python
import jax, jax.numpy as jnp
from jax import lax
import numpy as np

D_MODEL = 1024
BATCH = 8
SEQ = 2048
DEPTH = 4

CHUNK = 64
RMS_EPS = 1e-6
DN_HEADS = 4
DN_DK = 128
DN_DV = 128
DN_CONV = 4
DN_QKV = 2 * DN_HEADS * DN_DK + DN_HEADS * DN_DV
DN_VAL = DN_HEADS * DN_DV
RET_HEADS = 4
RET_DK = 128
RET_DV = 128
RET_QK = RET_HEADS * RET_DK
RET_VAL = RET_HEADS * RET_DV
ROPE_BASE = 10000.0
GN_EPS = 1e-5
POOL_WINDOWS = (2, 4, 8, 16)
POOL_GROUPS = 4
POOL_GROUP = 128
POOL_WIDTH = POOL_GROUPS * POOL_GROUP
N_BRANCH = 3
D_FF = -(-8 * D_MODEL // (3 * 256)) * 256
IN_SIZES = (DN_QKV, DN_VAL, DN_HEADS, DN_HEADS, RET_QK, RET_QK, RET_VAL, RET_VAL, POOL_WIDTH, N_BRANCH * D_MODEL)
D_IN = sum(IN_SIZES)

kernel_name = "hybrid_deltanet_pool_retention_block"


def _split_points(sizes):
    pts, acc = [], 0
    for s in sizes[:-1]:
        acc += s
        pts.append(acc)
    return pts


def rms_norm(x, w):
    xf = x.astype(jnp.float32)
    y = xf * lax.rsqrt(jnp.mean(xf * xf, axis=-1, keepdims=True) + RMS_EPS) * w.astype(jnp.float32)
    return y.astype(x.dtype)


def l2_normalize(x):
    return x * lax.rsqrt(jnp.sum(x * x, axis=-1, keepdims=True) + 1e-6)


def causal_depthwise_conv(x, w):
    k = w.shape[0]
    return lax.conv_general_dilated(
        x, w[:, None, :].astype(x.dtype), window_strides=(1,), padding=((k - 1, 0),),
        dimension_numbers=("NWC", "WIO", "NWC"), feature_group_count=x.shape[-1])


def to_chunks(x):
    b, s, h, d = x.shape
    return x.reshape(b, s // CHUNK, CHUNK, h, d).transpose(0, 3, 1, 2, 4)


def from_chunks(x):
    b, h, n, c, d = x.shape
    return x.transpose(0, 2, 3, 1, 4).reshape(b, n * c, h * d)


def gated_delta_rule(q, k, v, beta, g):
    b, s, h, dk = q.shape
    dv = v.shape[-1]
    n = s // CHUNK
    q = to_chunks(q) * (dk ** -0.5)
    k = to_chunks(k)
    v = to_chunks(v)
    beta = beta.reshape(b, n, CHUNK, h).transpose(0, 3, 1, 2)
    g = g.reshape(b, n, CHUNK, h).transpose(0, 3, 1, 2)
    G = jnp.cumsum(g, axis=-1)
    causal = jnp.tril(jnp.ones((CHUNK, CHUNK), dtype=bool))
    strict = jnp.tril(jnp.ones((CHUNK, CHUNK), dtype=bool), k=-1)
    diff = G[..., :, None] - G[..., None, :]
    decay = jnp.where(causal, jnp.exp(jnp.where(causal, diff, 0.0)), 0.0)
    k_beta = k * beta[..., None]
    m = jnp.where(strict, jnp.einsum('bhnid,bhnjd->bhnij', k_beta, k) * decay, 0.0)
    a = m + jnp.eye(CHUNK, dtype=m.dtype)
    rhs = jnp.concatenate([v * beta[..., None], k_beta * jnp.exp(G)[..., None]], axis=-1)
    sol = lax.linalg.triangular_solve(a, rhs, left_side=True, lower=True, unit_diagonal=True)
    u, w = sol[..., :dv], sol[..., dv:]
    attn = jnp.einsum('bhnid,bhnjd->bhnij', q, k) * decay
    g_last = G[..., -1]
    q_dec = q * jnp.exp(G)[..., None]
    k_tail = k * jnp.exp(g_last[..., None] - G)[..., None]
    xs = tuple(jnp.moveaxis(t, 2, 0) for t in (u, w, attn, q_dec, k_tail, jnp.exp(g_last)))

    def step(state, inp):
        u_c, w_c, a_c, qd_c, kt_c, dl_c = inp
        v_new = u_c - jnp.einsum('bhck,bhkv->bhcv', w_c, state)
        o = jnp.einsum('bhck,bhkv->bhcv', qd_c, state) + jnp.einsum('bhcj,bhjv->bhcv', a_c, v_new)
        state = state * dl_c[..., None, None] + jnp.einsum('bhck,bhcv->bhkv', kt_c, v_new)
        return state, o

    s0 = jnp.zeros((b, h, dk, dv), dtype=q.dtype)
    _, o = lax.scan(step, s0, xs)
    return from_chunks(jnp.moveaxis(o, 0, 2))


def rotary(x):
    s, d = x.shape[1], x.shape[-1]
    half = d // 2
    inv = ROPE_BASE ** (-jnp.arange(half, dtype=jnp.float32) / half)
    ang = jnp.arange(s, dtype=jnp.float32)[:, None] * inv[None, :]
    cos = jnp.cos(ang)[None, :, None, :]
    sin = jnp.sin(ang)[None, :, None, :]
    x1, x2 = x[..., :half], x[..., half:]
    return jnp.concatenate([x1 * cos - x2 * sin, x2 * cos + x1 * sin], axis=-1)


def chunkwise_retention(q, k, v, log_gamma):
    b, s, h, dk = q.shape
    dv = v.shape[-1]
    q = to_chunks(q)
    k = to_chunks(k) * (dk ** -0.5)
    v = to_chunks(v)
    idx = jnp.arange(CHUNK, dtype=jnp.float32)
    lg = log_gamma[:, None]
    dmask = jnp.exp(jnp.abs(idx[:, None] - idx[None, :])[None] * log_gamma[:, None, None])
    scores = jnp.einsum('bhnid,bhnjd->bhnij', q, k) * dmask[:, None]
    o_inner = jnp.einsum('bhnij,bhnjv->bhniv', scores, v)
    xi = jnp.exp((idx + 1.0)[None] * lg)
    zeta = jnp.exp((CHUNK - 1.0 - idx)[None] * lg)
    kv = jnp.einsum('bhnjd,bhnjv->bhndv', k * zeta[:, None, :, None], v)
    chunk_decay = jnp.exp(CHUNK * log_gamma)

    def step(r, kv_c):
        return r * chunk_decay[:, None, None] + kv_c, r

    _, r_prev = lax.scan(step, jnp.zeros((b, h, dk, dv), dtype=kv.dtype), jnp.moveaxis(kv, 2, 0))
    r_prev = jnp.moveaxis(r_prev, 0, 2)
    o_cross = jnp.einsum('bhnid,bhndv->bhniv', q * xi[:, None, :, None], r_prev)
    o = from_chunks(o_inner + o_cross)
    return o.reshape(b, s, h, dv)


def multiscale_pool(u, w_lin, scale):
    b, s, _ = u.shape
    groups = u.reshape(b, s, POOL_GROUPS, POOL_GROUP)
    cs = jnp.cumsum(groups, axis=1)
    t = jnp.arange(1, s + 1, dtype=jnp.float32)
    pooled = []
    for gi, win in enumerate(POOL_WINDOWS):
        c = cs[:, :, gi]
        prev = jnp.pad(c, ((0, 0), (win, 0), (0, 0)))[:, :s]
        cnt = jnp.minimum(t, float(win))[None, :, None]
        pooled.append((c - prev) / cnt)
    mixed = jnp.stack(pooled, axis=2) - groups
    y = jnp.einsum('bsgc,gcd->bsgd', mixed, w_lin.astype(jnp.float32)).reshape(b, s, POOL_WIDTH)
    return y * scale.astype(jnp.float32)


def hybrid_mixer(h, w_in, dn_conv, dn_A_log, dn_dt_bias, dn_out_norm, ret_out_norm,
                 pool_w, pool_scale, w_branch_dn, w_branch_ret, w_branch_pool, w_out):
    b, s, _ = h.shape
    f32 = jnp.float32
    proj = h @ w_in
    (dn_qkv, dn_z, dn_b, dn_a, ret_q, ret_k, ret_v, ret_g, pool_u, gates) = jnp.split(
        proj, _split_points(IN_SIZES), axis=-1)

    qkv = jax.nn.silu(causal_depthwise_conv(dn_qkv, dn_conv)).astype(f32)
    q, k, v = jnp.split(qkv, [DN_HEADS * DN_DK, 2 * DN_HEADS * DN_DK], axis=-1)
    q = l2_normalize(q.reshape(b, s, DN_HEADS, DN_DK))
    k = l2_normalize(k.reshape(b, s, DN_HEADS, DN_DK))
    v = v.reshape(b, s, DN_HEADS, DN_DV)
    beta = jax.nn.sigmoid(dn_b.astype(f32))
    g = -jnp.exp(dn_A_log.astype(f32)) * jax.nn.softplus(dn_a.astype(f32) + dn_dt_bias.astype(f32))
    o_dn = gated_delta_rule(q, k, v, beta, g).reshape(b, s, DN_HEADS, DN_DV)
    o_dn = o_dn * lax.rsqrt(jnp.mean(o_dn * o_dn, axis=-1, keepdims=True) + RMS_EPS) * dn_out_norm.astype(f32)
    o_dn = o_dn.reshape(b, s, DN_VAL) * jax.nn.silu(dn_z.astype(f32))

    o_pool = multiscale_pool(pool_u.astype(f32), pool_w, pool_scale)

    log_gamma = jnp.log1p(-jnp.exp2(-5.0 - jnp.arange(RET_HEADS, dtype=f32)))
    rq = rotary(ret_q.astype(f32).reshape(b, s, RET_HEADS, RET_DK))
    rk = rotary(ret_k.astype(f32).reshape(b, s, RET_HEADS, RET_DK))
    rv = ret_v.astype(f32).reshape(b, s, RET_HEADS, RET_DV)
    o_ret = chunkwise_retention(rq, rk, rv, log_gamma)
    mu = jnp.mean(o_ret, axis=-1, keepdims=True)
    var = jnp.mean(jnp.square(o_ret - mu), axis=-1, keepdims=True)
    o_ret = ((o_ret - mu) * lax.rsqrt(var + GN_EPS)).reshape(b, s, RET_VAL) * ret_out_norm.astype(f32)
    o_ret = o_ret * jax.nn.silu(ret_g.astype(f32))

    gt = jax.nn.sigmoid(gates.astype(f32)).reshape(b, s, N_BRANCH, D_MODEL)
    y = (gt[:, :, 0] * (o_dn.astype(h.dtype) @ w_branch_dn)
         + gt[:, :, 1] * (o_pool.astype(h.dtype) @ w_branch_pool)
         + gt[:, :, 2] * (o_ret.astype(h.dtype) @ w_branch_ret))
    return y.astype(h.dtype) @ w_out


def swiglu(h, w_gate, w_up, w_down):
    return (jax.nn.silu(h @ w_gate) * (h @ w_up)) @ w_down


def setup_inputs(seed: int = 0) -> dict:
    key = jax.random.key(seed)
    ks = jax.random.split(key, 24)
    f32 = jnp.float32
    L = DEPTH

    def nrm(k, shape, fan_in):
        return jax.random.normal(k, shape, f32) * (fan_in ** -0.5)

    def gain(k, shape):
        return 1.0 + 0.1 * jax.random.normal(k, shape, f32)

    dt = jnp.exp(jax.random.uniform(ks[6], (L, DN_HEADS), f32, minval=float(np.log(1e-3)), maxval=float(np.log(1e-1))))
    return {
        "x": jax.random.normal(ks[0], (BATCH, SEQ, D_MODEL), f32),
        "mix_pre_norm": gain(ks[1], (L, D_MODEL)),
        "mix_post_norm": gain(ks[2], (L, D_MODEL)),
        "w_in": nrm(ks[3], (L, D_MODEL, D_IN), D_MODEL),
        "dn_conv": nrm(ks[4], (L, DN_CONV, DN_QKV), DN_CONV),
        "dn_A_log": jnp.log(jax.random.uniform(ks[5], (L, DN_HEADS), f32, minval=1.0, maxval=16.0)),
        "dn_dt_bias": dt + jnp.log(-jnp.expm1(-dt)),
        "dn_out_norm": gain(ks[7], (L, DN_DV)),
        "ret_out_norm": gain(ks[8], (L, RET_VAL)),
        "pool_w": nrm(ks[9], (L, POOL_GROUPS, POOL_GROUP, POOL_GROUP), POOL_GROUP),
        "pool_scale": gain(ks[10], (L, POOL_WIDTH)),
        "w_branch_dn": nrm(ks[11], (L, DN_VAL, D_MODEL), DN_VAL),
        "w_branch_ret": nrm(ks[12], (L, RET_VAL, D_MODEL), RET_VAL),
        "w_branch_pool": nrm(ks[13], (L, POOL_WIDTH, D_MODEL), POOL_WIDTH),
        "w_out": nrm(ks[14], (L, D_MODEL, D_MODEL), D_MODEL),
        "ffn_pre_norm": gain(ks[15], (L, D_MODEL)),
        "ffn_post_norm": gain(ks[16], (L, D_MODEL)),
        "ffn_gate": nrm(ks[17], (L, D_MODEL, D_FF), D_MODEL),
        "ffn_up": nrm(ks[18], (L, D_MODEL, D_FF), D_MODEL),
        "ffn_down": nrm(ks[19], (L, D_FF, D_MODEL), D_FF),
    }


def reference(x, mix_pre_norm, mix_post_norm, w_in, dn_conv, dn_A_log, dn_dt_bias, dn_out_norm,
              ret_out_norm, pool_w, pool_scale, w_branch_dn, w_branch_ret, w_branch_pool, w_out,
              ffn_pre_norm, ffn_post_norm, ffn_gate, ffn_up, ffn_down):
    for l in range(DEPTH):
        h = rms_norm(x, mix_pre_norm[l])
        m = hybrid_mixer(h, w_in[l], dn_conv[l], dn_A_log[l], dn_dt_bias[l], dn_out_norm[l],
                         ret_out_norm[l], pool_w[l], pool_scale[l], w_branch_dn[l], w_branch_ret[l],
                         w_branch_pool[l], w_out[l])
        x = x + rms_norm(m, mix_post_norm[l])
        h = rms_norm(x, ffn_pre_norm[l])
        x = x + rms_norm(swiglu(h, ffn_gate[l], ffn_up[l], ffn_down[l]), ffn_post_norm[l])
    return x
```

```python
import functools
import math

import jax
import jax.numpy as jnp
from jax import lax
from jax.experimental import pallas as pl
from jax.experimental.pallas import tpu as pltpu

F32 = jnp.float32
BF16 = jnp.bfloat16

D_MODEL = 1024
CHUNK = 64
RMS_EPS = 1e-6
GN_EPS = 1e-5
L2_EPS = 1e-6
N_HEADS = 4
HEAD_DIM = 128
DN_QKV = 3 * N_HEADS * HEAD_DIM
WIDTH = N_HEADS * HEAD_DIM
DN_CONV = 4
ROPE_BASE = 10000.0
POOL_WINDOWS = (2, 4, 8, 16)
LANES = 128
CONV_HIST = 8
POOL_HIST = 16

C_QKV = 0
C_DNZ = C_QKV + DN_QKV
C_RQ = C_DNZ + WIDTH
C_RK = C_RQ + WIDTH
C_RV = C_RK + WIDTH
C_RG = C_RV + WIDTH
C_PU = C_RG + WIDTH
C_GATES = C_PU + WIDTH
N_MAIN = C_GATES + 3 * D_MODEL
G_LANE = N_HEADS

VMEM_LIMIT = 56 * 1024 * 1024


def _dot(a, b):
    return jnp.dot(a, b, preferred_element_type=F32)


def _dot_nt(a, b):
    return lax.dot_general(a, b, (((1,), (1,)), ((), ())), preferred_element_type=F32)


def _dot_tn(a, b):
    return lax.dot_general(a, b, (((0,), (0,)), ((), ())), preferred_element_type=F32)


def _rms(xf, w):
    return xf * lax.rsqrt(jnp.mean(xf * xf, axis=-1, keepdims=True) + RMS_EPS) * w


def _silu(x):
    return x * jax.nn.sigmoid(x)


def _proj_body(x_ref, nw_ref, w_ref, ws_ref, out_ref, small_ref, h_scr):
    @pl.when(pl.program_id(1) == 0)
    def _():
        h = _rms(x_ref[...], nw_ref[...]).astype(BF16)
        h_scr[...] = h
        small_ref[...] = _dot(h, ws_ref[...])

    out_ref[...] = _dot(h_scr[...], w_ref[...]).astype(out_ref.dtype)


def _proj(x, nw, w_main, w_small, *, tm, tn, out_dtype):
    t, d = x.shape
    n = w_main.shape[1]
    return pl.pallas_call(
        _proj_body,
        grid=(t // tm, n // tn),
        in_specs=[
            pl.BlockSpec((tm, d), lambda i, j: (i, 0)),
            pl.BlockSpec((1, d), lambda i, j: (0, 0)),
            pl.BlockSpec((d, tn), lambda i, j: (0, j)),
            pl.BlockSpec((d, LANES), lambda i, j: (0, 0)),
        ],
        out_specs=[
            pl.BlockSpec((tm, tn), lambda i, j: (i, j)),
            pl.BlockSpec((tm, LANES), lambda i, j: (i, 0)),
        ],
        out_shape=[
            jax.ShapeDtypeStruct((t, n), out_dtype),
            jax.ShapeDtypeStruct((t, LANES), F32),
        ],
        scratch_shapes=[pltpu.VMEM((tm, d), BF16)],
        compiler_params=pltpu.CompilerParams(
            dimension_semantics=("arbitrary", "arbitrary"), vmem_limit_bytes=VMEM_LIMIT),
        name="proj",
    )(x, nw, w_main, w_small)


def _inv_unit_lower(m_strict, eye):
    p = -m_strict
    t = eye + p
    for _ in range(5):
        p16 = p.astype(BF16)
        p = _dot(p16, p16)
        t = t + _dot(t.astype(BF16), p.astype(BF16))
    return t


def _mix_body(main_ref, small_ref, cos_ref, sin_ref, conv_ref, avec_ref, dtb_ref, dnw_ref,
              retw_ref, poolw_ref, pscale_ref, ocat_ref,
              s_scr, r_scr, chist, phist, cbuf, pbuf, qkv_scr, rq_scr, rk_scr,
              beta_scr, gcol_scr, grow_scr):
    t_idx = pl.program_id(1)
    ts = main_ref.shape[0]
    nc = ts // CHUNK

    @pl.when(t_idx == 0)
    def _():
        s_scr[...] = jnp.zeros_like(s_scr)
        r_scr[...] = jnp.zeros_like(r_scr)
        chist[...] = jnp.zeros_like(chist)
        phist[...] = jnp.zeros_like(phist)

    cbuf[0:CONV_HIST, :] = chist[...]
    cbuf[CONV_HIST:CONV_HIST + ts, :] = main_ref[:, C_QKV:C_QKV + DN_QKV].astype(F32)
    chist[...] = cbuf[ts:ts + CONV_HIST, :]
    base = CONV_HIST - (DN_CONV - 1)
    for j in range(DN_QKV // LANES):
        cs = slice(j * LANES, (j + 1) * LANES)
        acc = conv_ref[0:1, cs] * cbuf[base:base + ts, cs]
        for k in range(1, DN_CONV):
            acc = acc + conv_ref[k:k + 1, cs] * cbuf[base + k:base + k + ts, cs]
        y = _silu(acc)
        if j < 2 * N_HEADS:
            y = y * lax.rsqrt(jnp.sum(y * y, axis=-1, keepdims=True) + L2_EPS)
        if j < N_HEADS:
            y = y * (HEAD_DIM ** -0.5)
        qkv_scr[:, cs] = y

    sm = small_ref[...]
    beta_scr[...] = jax.nn.sigmoid(sm)
    g = -(jnp.exp(avec_ref[...]) * jax.nn.softplus(sm + dtb_ref[...]))
    ii = lax.broadcasted_iota(jnp.int32, (ts, ts), 0)
    jj = lax.broadcasted_iota(jnp.int32, (ts, ts), 1)
    same_chunk = lax.shift_right_logical(ii, 6) == lax.shift_right_logical(jj, 6)
    lblk = jnp.where(same_chunk & (jj <= ii), 1.0, 0.0).astype(F32)
    gcol = jnp.dot(lblk, g, precision=lax.Precision.HIGHEST, preferred_element_type=F32)
    gcol_scr[...] = gcol
    grow_scr[...] = gcol.T[0:8, :]

    cos = cos_ref[...]
    sin = sin_ref[...]
    for h in range(N_HEADS):
        cs = slice(h * LANES, (h + 1) * LANES)
        xq = main_ref[:, C_RQ + h * LANES:C_RQ + (h + 1) * LANES].astype(F32)
        rq_scr[:, cs] = xq * cos + pltpu.roll(xq, HEAD_DIM // 2, 1) * sin
        xk = main_ref[:, C_RK + h * LANES:C_RK + (h + 1) * LANES].astype(F32)
        rk_scr[:, cs] = (xk * cos + pltpu.roll(xk, HEAD_DIM // 2, 1) * sin) * (HEAD_DIM ** -0.5)

    pbuf[0:POOL_HIST, :] = phist[...]
    pbuf[POOL_HIST:POOL_HIST + ts, :] = main_ref[:, C_PU:C_PU + WIDTH].astype(F32)
    phist[...] = pbuf[ts:ts + POOL_HIST, :]
    pos = (lax.broadcasted_iota(jnp.int32, (ts, LANES), 0) + t_idx * ts + 1).astype(F32)
    for gi, win in enumerate(POOL_WINDOWS):
        cs = slice(gi * LANES, (gi + 1) * LANES)
        xs = pbuf[POOL_HIST:POOL_HIST + ts, cs]
        wsum = xs
        for k in range(1, win):
            wsum = wsum + pbuf[POOL_HIST - k:POOL_HIST - k + ts, cs]
        mixed = wsum / jnp.minimum(pos, float(win)) - xs
        y = _dot(mixed.astype(BF16), poolw_ref[gi]) * pscale_ref[0:1, cs]
        ocat_ref[:, WIDTH + gi * LANES:WIDTH + (gi + 1) * LANES] = y.astype(ocat_ref.dtype)

    ri = lax.broadcasted_iota(jnp.int32, (CHUNK, CHUNK), 0)
    ci = lax.broadcasted_iota(jnp.int32, (CHUNK, CHUNK), 1)
    causal = ri >= ci
    strict = ri > ci
    eye = jnp.where(ri == ci, 1.0, 0.0).astype(F32)
    absd = jnp.abs(ri - ci).astype(F32)
    rowf = lax.broadcasted_iota(jnp.int32, (CHUNK, LANES), 0).astype(F32)
    ret_consts = []
    for h in range(N_HEADS):
        lg = math.log1p(-(2.0 ** (-5.0 - h)))
        ret_consts.append((jnp.exp(absd * lg), jnp.exp((rowf + 1.0) * lg),
                           jnp.exp((CHUNK - 1.0 - rowf) * lg), math.exp(CHUNK * lg)))

    for c in range(nc):
        rs = slice(c * CHUNK, (c + 1) * CHUNK)
        gc = gcol_scr[rs, :]
        gl = gc[CHUNK - 1:CHUNK, :]
        ktf = jnp.exp(gl - gc)
        dl = jnp.exp(gl)
        eg = jnp.exp(gc)
        bt = beta_scr[rs, :]
        for h in range(N_HEADS):
            hs = slice(h * LANES, (h + 1) * LANES)
            col = slice(G_LANE + h, G_LANE + h + 1)
            q = qkv_scr[rs, h * LANES:(h + 1) * LANES]
            k = qkv_scr[rs, WIDTH + h * LANES:WIDTH + (h + 1) * LANES]
            v = qkv_scr[rs, 2 * WIDTH + h * LANES:2 * WIDTH + (h + 1) * LANES]
            diff = gc[:, col] - grow_scr[G_LANE + h:G_LANE + h + 1, rs]
            decay = jnp.where(causal, jnp.exp(jnp.where(causal, diff, 0.0)), 0.0)
            beta_h = bt[:, h:h + 1]
            kb = k * beta_h
            k16 = k.astype(BF16)
            a_kk = _dot_nt(kb.astype(BF16), k16)
            t_inv = _inv_unit_lower(jnp.where(strict, a_kk * decay, 0.0), eye)
            rhs = jnp.concatenate([v * beta_h, kb * eg[:, col]], axis=-1).astype(BF16)
            sol = _dot(t_inv.astype(BF16), rhs)
            u = sol[:, :HEAD_DIM]
            w = sol[:, HEAD_DIM:]
            attn = _dot_nt(q.astype(BF16), k16) * decay
            state = s_scr[h]
            s16 = state.astype(BF16)
            v_new = u - _dot(w.astype(BF16), s16)
            vn16 = v_new.astype(BF16)
            o = _dot((q * eg[:, col]).astype(BF16), s16) + _dot(attn.astype(BF16), vn16)
            s_scr[h] = state * dl[:, col] + _dot_tn((k * ktf[:, col]).astype(BF16), vn16)
            o = o * lax.rsqrt(jnp.mean(o * o, axis=-1, keepdims=True) + RMS_EPS) * dnw_ref[...]
            z = main_ref[rs, C_DNZ + h * LANES:C_DNZ + (h + 1) * LANES].astype(F32)
            ocat_ref[rs, hs] = (o * _silu(z)).astype(ocat_ref.dtype)

            dmask, xi, zeta, cdec = ret_consts[h]
            rq = rq_scr[rs, hs]
            rk = rk_scr[rs, hs]
            rv16 = main_ref[rs, C_RV + h * LANES:C_RV + (h + 1) * LANES].astype(BF16)
            rk16 = rk.astype(BF16)
            scores = _dot_nt(rq.astype(BF16), rk16) * dmask
            rstate = r_scr[h]
            o_r = _dot(scores.astype(BF16), rv16) + _dot((rq * xi).astype(BF16), rstate.astype(BF16))
            r_scr[h] = rstate * cdec + _dot_tn((rk * zeta).astype(BF16), rv16)
            mu = jnp.mean(o_r, axis=-1, keepdims=True)
            dev = o_r - mu
            var = jnp.mean(dev * dev, axis=-1, keepdims=True)
            o_r = dev * lax.rsqrt(var + GN_EPS) * retw_ref[0:1, hs]
            gr = main_ref[rs, C_RG + h * LANES:C_RG + (h + 1) * LANES].astype(F32)
            ocat_ref[rs, 2 * WIDTH + h * LANES:2 * WIDTH + (h + 1) * LANES] = (
                o_r * _silu(gr)).astype(ocat_ref.dtype)


def _mix(proj, small, cos, sin, conv, avec, dtb, dnw, retw, poolw, pscale, *, batch, seq, ts):
    nt = seq // ts
    mix_w = C_GATES
    row = lambda b, t: (b * nt + t, 0)
    const2 = lambda b, t: (0, 0)
    return pl.pallas_call(
        _mix_body,
        grid=(batch, nt),
        in_specs=[
            pl.BlockSpec((ts, mix_w), row),
            pl.BlockSpec((ts, LANES), row),
            pl.BlockSpec((ts, LANES), lambda b, t: (t, 0)),
            pl.BlockSpec((ts, LANES), lambda b, t: (t, 0)),
            pl.BlockSpec((DN_CONV, DN_QKV), const2),
            pl.BlockSpec((1, LANES), const2),
            pl.BlockSpec((1, LANES), const2),
            pl.BlockSpec((1, LANES), const2),
            pl.BlockSpec((1, WIDTH), const2),
            pl.BlockSpec((N_HEADS, LANES, LANES), lambda b, t: (0, 0, 0)),
            pl.BlockSpec((1, WIDTH), const2),
        ],
        out_specs=pl.BlockSpec((ts, 3 * WIDTH), row),
        out_shape=jax.ShapeDtypeStruct((batch * seq, 3 * WIDTH), BF16),
        scratch_shapes=[
            pltpu.VMEM((N_HEADS, HEAD_DIM, HEAD_DIM), F32),
            pltpu.VMEM((N_HEADS, HEAD_DIM, HEAD_DIM), F32),
            pltpu.VMEM((CONV_HIST, DN_QKV), F32),
            pltpu.VMEM((POOL_HIST, WIDTH), F32),
            pltpu.VMEM((CONV_HIST + ts, DN_QKV), F32),
            pltpu.VMEM((POOL_HIST + ts, WIDTH), F32),
            pltpu.VMEM((ts, DN_QKV), F32),
            pltpu.VMEM((ts, WIDTH), F32),
            pltpu.VMEM((ts, WIDTH), F32),
            pltpu.VMEM((ts, LANES), F32),
            pltpu.VMEM((ts, LANES), F32),
            pltpu.VMEM((8, ts), F32),
        ],
        compiler_params=pltpu.CompilerParams(
            dimension_semantics=("arbitrary", "arbitrary"), vmem_limit_bytes=VMEM_LIMIT),
        name="mix",
    )(proj, small, cos, sin, conv, avec, dtb, dnw, retw, poolw, pscale)


def _merge_body(x_ref, ocat_ref, ga_ref, gb_ref, wdn_ref, wpool_ref, wret_ref, wout_ref, pw_ref, out_ref):
    oc = ocat_ref[...]
    d = _dot(oc[:, 0:WIDTH], wdn_ref[...])
    p = _dot(oc[:, WIDTH:2 * WIDTH], wpool_ref[...])
    r = _dot(oc[:, 2 * WIDTH:3 * WIDTH], wret_ref[...])
    ga = jax.nn.sigmoid(ga_ref[...].astype(F32))
    gb = jax.nn.sigmoid(gb_ref[...].astype(F32))
    half = D_MODEL // 2
    y_lo = (ga[:, 0:half] * d[:, 0:half] + ga[:, D_MODEL:D_MODEL + half] * p[:, 0:half]
            + gb[:, half:D_MODEL] * r[:, 0:half])
    y_hi = (ga[:, half:D_MODEL] * d[:, half:] + gb[:, 0:half] * p[:, half:]
            + gb[:, D_MODEL:D_MODEL + half] * r[:, half:])
    y = jnp.concatenate([y_lo, y_hi], axis=-1).astype(BF16)
    m = _dot(y, wout_ref[...])
    out_ref[...] = x_ref[...] + _rms(m, pw_ref[...])


def _merge(x, ocat, proj, wdn, wpool, wret, wout, pw, *, tm):
    t, d = x.shape
    gw = 3 * D_MODEL // 2
    gblk = C_GATES // gw
    const2 = lambda i: (0, 0)
    return pl.pallas_call(
        _merge_body,
        grid=(t // tm,),
        in_specs=[
            pl.BlockSpec((tm, d), lambda i: (i, 0)),
            pl.BlockSpec((tm, 3 * WIDTH), lambda i: (i, 0)),
            pl.BlockSpec((tm, gw), lambda i: (i, gblk)),
            pl.BlockSpec((tm, gw), lambda i: (i, gblk + 1)),
            pl.BlockSpec((WIDTH, d), const2),
            pl.BlockSpec((WIDTH, d), const2),
            pl.BlockSpec((WIDTH, d), const2),
            pl.BlockSpec((d, d), const2),
            pl.BlockSpec((1, d), const2),
        ],
        out_specs=pl.BlockSpec((tm, d), lambda i: (i, 0)),
        out_shape=jax.ShapeDtypeStruct((t, d), F32),
        compiler_params=pltpu.CompilerParams(
            dimension_semantics=("arbitrary",), vmem_limit_bytes=VMEM_LIMIT),
        name="merge",
    )(x, ocat, proj, proj, wdn, wpool, wret, wout, pw)


def _ffn_body(x_ref, prew_ref, wg_ref, wu_ref, wd_ref, postw_ref, out_ref, h_scr, acc_scr):
    j = pl.program_id(1)

    @pl.when(j == 0)
    def _():
        h_scr[...] = _rms(x_ref[...], prew_ref[...]).astype(BF16)
        acc_scr[...] = jnp.zeros_like(acc_scr)

    h = h_scr[...]
    a = (_silu(_dot(h, wg_ref[...])) * _dot(h, wu_ref[...])).astype(BF16)
    acc_scr[...] += _dot(a, wd_ref[...])

    @pl.when(j == pl.num_programs(1) - 1)
    def _():
        out_ref[...] = x_ref[...] + _rms(acc_scr[...], postw_ref[...])


def _ffn(x, prew, wg, wu, wd, postw, *, tm, tf):
    t, d = x.shape
    f = wg.shape[1]
    return pl.pallas_call(
        _ffn_body,
        grid=(t // tm, f // tf),
        in_specs=[
            pl.BlockSpec((tm, d), lambda i, j: (i, 0)),
            pl.BlockSpec((1, d), lambda i, j: (0, 0)),
            pl.BlockSpec((d, tf), lambda i, j: (0, j)),
            pl.BlockSpec((d, tf), lambda i, j: (0, j)),
            pl.BlockSpec((tf, d), lambda i, j: (j, 0)),
            pl.BlockSpec((1, d), lambda i, j: (0, 0)),
        ],
        out_specs=pl.BlockSpec((tm, d), lambda i, j: (i, 0)),
        out_shape=jax.ShapeDtypeStruct((t, d), F32),
        scratch_shapes=[pltpu.VMEM((tm, d), BF16), pltpu.VMEM((tm, d), F32)],
        compiler_params=pltpu.CompilerParams(
            dimension_semantics=("arbitrary", "arbitrary"), vmem_limit_bytes=VMEM_LIMIT),
        name="ffn",
    )(x, prew, wg, wu, wd, postw)


def _rope_tables(seq):
    half = HEAD_DIM // 2
    inv = ROPE_BASE ** (-jnp.arange(half, dtype=F32) / half)
    ang = jnp.arange(seq, dtype=F32)[:, None] * inv[None, :]
    cos = jnp.cos(ang)
    sin = jnp.sin(ang)
    return jnp.concatenate([cos, cos], axis=-1), jnp.concatenate([-sin, sin], axis=-1)


def _lane_vec(vals):
    return jnp.zeros((1, LANES), F32).at[0, G_LANE:G_LANE + N_HEADS].set(vals.astype(F32))


def kernel(x, mix_pre_norm, mix_post_norm, w_in, dn_conv, dn_A_log, dn_dt_bias, dn_out_norm, ret_out_norm, pool_w, pool_scale, w_branch_dn, w_branch_ret, w_branch_pool, w_out, ffn_pre_norm, ffn_post_norm, ffn_gate, ffn_up, ffn_down):
    batch, seq, d = x.shape
    depth = w_in.shape[0]
    t = batch * seq
    cos, sin = _rope_tables(seq)
    xt = x.reshape(t, d)
    small_lo = C_DNZ + WIDTH
    small_hi = small_lo + 2 * N_HEADS
    for l in range(depth):
        wl = w_in[l]
        w_main = jnp.concatenate([wl[:, :small_lo], wl[:, small_hi:]], axis=1).astype(BF16)
        w_small = jnp.pad(wl[:, small_lo:small_hi], ((0, 0), (0, LANES - 2 * N_HEADS))).astype(BF16)
        proj, small = _proj(xt, mix_pre_norm[l][None, :], w_main, w_small,
                            tm=1024, tn=768, out_dtype=F32)
        ocat = _mix(proj, small, cos, sin, dn_conv[l], _lane_vec(dn_A_log[l]),
                    _lane_vec(dn_dt_bias[l]), dn_out_norm[l][None, :], ret_out_norm[l][None, :],
                    pool_w[l].astype(BF16), pool_scale[l][None, :], batch=batch, seq=seq, ts=256)
        xt = _merge(xt, ocat, proj, w_branch_dn[l].astype(BF16), w_branch_pool[l].astype(BF16),
                    w_branch_ret[l].astype(BF16), w_out[l].astype(BF16), mix_post_norm[l][None, :],
                    tm=512)
        xt = _ffn(xt, ffn_pre_norm[l][None, :], ffn_gate[l].astype(BF16), ffn_up[l].astype(BF16),
                  ffn_down[l].astype(BF16), ffn_post_norm[l][None, :], tm=1024, tf=256)
    return xt.reshape(batch, seq, d)
```

```python
import functools
import math

import jax
import jax.numpy as jnp
from jax import lax
from jax.experimental import pallas as pl
from jax.experimental.pallas import tpu as pltpu

F32 = jnp.float32
BF16 = jnp.bfloat16

D_MODEL = 1024
CHUNK = 64
RMS_EPS = 1e-6
GN_EPS = 1e-5
L2_EPS = 1e-6
N_HEADS = 4
HEAD_DIM = 128
DN_QKV = 3 * N_HEADS * HEAD_DIM
WIDTH = N_HEADS * HEAD_DIM
DN_CONV = 4
ROPE_BASE = 10000.0
POOL_WINDOWS = (2, 4, 8, 16)
LANES = 128
CONV_HIST = 8
POOL_HIST = 16

C_QKV = 0
C_DNZ = C_QKV + DN_QKV
C_RQ = C_DNZ + WIDTH
C_RK = C_RQ + WIDTH
C_RV = C_RK + WIDTH
C_RG = C_RV + WIDTH
C_PU = C_RG + WIDTH
C_GATES = C_PU + WIDTH
N_MAIN = C_GATES + 3 * D_MODEL
G_LANE = N_HEADS

VMEM_LIMIT = 56 * 1024 * 1024


def _dot(a, b):
    return jnp.dot(a, b, preferred_element_type=F32)


def _dot_nt(a, b):
    return lax.dot_general(a, b, (((1,), (1,)), ((), ())), preferred_element_type=F32)


def _rms(xf, w):
    return xf * lax.rsqrt(jnp.mean(xf * xf, axis=-1, keepdims=True) + RMS_EPS) * w


def _silu(x):
    return x * jax.nn.sigmoid(x)


def _proj_body(x_ref, nw_ref, w_ref, ws_ref, out_ref, small_ref, h_scr):
    @pl.when(pl.program_id(1) == 0)
    def _():
        h = _rms(x_ref[...], nw_ref[...]).astype(BF16)
        h_scr[...] = h
        small_ref[...] = _dot(h, ws_ref[...])

    out_ref[...] = _dot(h_scr[...], w_ref[...]).astype(out_ref.dtype)


def _proj(x, nw, w_main, w_small, *, tm, tn, out_dtype):
    t, d = x.shape
    n = w_main.shape[1]
    return pl.pallas_call(
        _proj_body,
        grid=(t // tm, n // tn),
        in_specs=[
            pl.BlockSpec((tm, d), lambda i, j: (i, 0)),
            pl.BlockSpec((1, d), lambda i, j: (0, 0)),
            pl.BlockSpec((d, tn), lambda i, j: (0, j)),
            pl.BlockSpec((d, LANES), lambda i, j: (0, 0)),
        ],
        out_specs=[
            pl.BlockSpec((tm, tn), lambda i, j: (i, j)),
            pl.BlockSpec((tm, LANES), lambda i, j: (i, 0)),
        ],
        out_shape=[
            jax.ShapeDtypeStruct((t, n), out_dtype),
            jax.ShapeDtypeStruct((t, LANES), F32),
        ],
        scratch_shapes=[pltpu.VMEM((tm, d), BF16)],
        compiler_params=pltpu.CompilerParams(
            dimension_semantics=("arbitrary", "arbitrary"), vmem_limit_bytes=VMEM_LIMIT),
        name="proj",
    )(x, nw, w_main, w_small)


def _mix_body(main_ref, small_ref, cos_ref, sin_ref, conv_ref, alog_ref, dtb_ref, dnw_ref,
              retw_ref, poolw_ref, pscale_ref, ocat_ref,
              s_scr, r_scr, chist, phist, cbuf, pbuf, qkv_scr, rq_scr, rk_scr, *, group):
    t_idx = pl.program_id(0)
    nb, ts = main_ref.shape[0], main_ref.shape[1]
    assert ts == CHUNK

    @pl.when(t_idx == 0)
    def _():
        s_scr[...] = jnp.zeros_like(s_scr)
        r_scr[...] = jnp.zeros_like(r_scr)
        chist[...] = jnp.zeros_like(chist)
        phist[...] = jnp.zeros_like(phist)

    ri = lax.broadcasted_iota(jnp.int32, (CHUNK, CHUNK), 0)
    ci = lax.broadcasted_iota(jnp.int32, (CHUNK, CHUNK), 1)
    causal = ri >= ci
    strict = ri > ci
    eye = jnp.where(ri == ci, 1.0, 0.0).astype(F32)
    tril = jnp.where(causal, 1.0, 0.0).astype(F32)
    absd = jnp.abs(ri - ci).astype(F32)
    rowf = lax.broadcasted_iota(jnp.int32, (CHUNK, LANES), 0).astype(F32)
    pos = rowf + (t_idx * ts + 1).astype(F32)
    cos = cos_ref[...]
    sin = sin_ref[...]

    g_rows = []
    beta = []
    for b in range(nb):
        cbuf[b, 0:CONV_HIST, :] = chist[b]
        cbuf[b, CONV_HIST:CONV_HIST + ts, :] = main_ref[b, :, C_QKV:C_QKV + DN_QKV].astype(F32)
        chist[b] = cbuf[b, ts:ts + CONV_HIST, :]
        base = CONV_HIST - (DN_CONV - 1)
        for j in range(DN_QKV // LANES):
            cs = slice(j * LANES, (j + 1) * LANES)
            acc = conv_ref[0:1, cs] * cbuf[b, base:base + ts, cs]
            for k in range(1, DN_CONV):
                acc = acc + conv_ref[k:k + 1, cs] * cbuf[b, base + k:base + k + ts, cs]
            y = _silu(acc)
            if j < 2 * N_HEADS:
                y = y * lax.rsqrt(jnp.sum(y * y, axis=-1, keepdims=True) + L2_EPS)
            if j < N_HEADS:
                y = y * (HEAD_DIM ** -0.5)
            qkv_scr[b, :, cs] = y

        for h in range(N_HEADS):
            cs = slice(h * LANES, (h + 1) * LANES)
            xq = main_ref[b, :, C_RQ + h * LANES:C_RQ + (h + 1) * LANES].astype(F32)
            rq_scr[b, :, cs] = xq * cos + pltpu.roll(xq, HEAD_DIM // 2, 1) * sin
            xk = main_ref[b, :, C_RK + h * LANES:C_RK + (h + 1) * LANES].astype(F32)
            rk_scr[b, :, cs] = (xk * cos + pltpu.roll(xk, HEAD_DIM // 2, 1) * sin) * (HEAD_DIM ** -0.5)

        pbuf[b, 0:POOL_HIST, :] = phist[b]
        pbuf[b, POOL_HIST:POOL_HIST + ts, :] = main_ref[b, :, C_PU:C_PU + WIDTH].astype(F32)
        phist[b] = pbuf[b, ts:ts + POOL_HIST, :]
        for gi, win in enumerate(POOL_WINDOWS):
            cs = slice(gi * LANES, (gi + 1) * LANES)
            xs = pbuf[b, POOL_HIST:POOL_HIST + ts, cs]
            wsum = xs
            for k in range(1, win):
                wsum = wsum + pbuf[b, POOL_HIST - k:POOL_HIST - k + ts, cs]
            mixed = wsum / jnp.minimum(pos, float(win)) - xs
            y = _dot(mixed.astype(BF16), poolw_ref[gi]) * pscale_ref[0:1, cs]
            ocat_ref[b, :, WIDTH + gi * LANES:WIDTH + (gi + 1) * LANES] = y.astype(ocat_ref.dtype)

        sm = small_ref[b]
        beta.append(jax.nn.sigmoid(sm))
        g_rows.append(-(jnp.exp(alog_ref[...]) * jax.nn.softplus(sm + dtb_ref[...])))

    gcol_all = jnp.dot(tril, jnp.concatenate(g_rows, axis=-1),
                       precision=lax.Precision.HIGHEST, preferred_element_type=F32)
    gcol, grow, eg, ktf, dl = [], [], [], [], []
    for b in range(nb):
        gc = gcol_all[:, b * LANES:(b + 1) * LANES]
        gl = gc[CHUNK - 1:CHUNK, :]
        gcol.append(gc)
        grow.append(gc.T[0:2 * N_HEADS, :])
        eg.append(jnp.exp(gc))
        ktf.append(jnp.exp(gl - gc))
        dl.append(jnp.exp(gl))

    ret_consts = []
    for h in range(N_HEADS):
        lg = math.log1p(-(2.0 ** (-5.0 - h)))
        ret_consts.append((jnp.exp(absd * lg), jnp.exp((rowf + 1.0) * lg),
                           jnp.exp((CHUNK - 1.0 - rowf) * lg), math.exp(CHUNK * lg)))

    streams = [(b, h) for b in range(nb) for h in range(N_HEADS)]
    for g0 in range(0, len(streams), group):
        grp = streams[g0:g0 + group]

        st = []
        for b, h in grp:
            col = slice(G_LANE + h, G_LANE + h + 1)
            q = qkv_scr[b, :, h * LANES:(h + 1) * LANES]
            k = qkv_scr[b, :, WIDTH + h * LANES:WIDTH + (h + 1) * LANES]
            v = qkv_scr[b, :, 2 * WIDTH + h * LANES:2 * WIDTH + (h + 1) * LANES]
            diff = gcol[b][:, col] - grow[b][G_LANE + h:G_LANE + h + 1, :]
            decay = jnp.where(causal, jnp.exp(jnp.where(causal, diff, 0.0)), 0.0)
            beta_h = beta[b][:, h:h + 1]
            kb = k * beta_h
            egc = eg[b][:, col]
            kq = _dot_nt(jnp.concatenate([kb, q], axis=0).astype(BF16), k.astype(BF16))
            p = -jnp.where(strict, kq[0:CHUNK] * decay, 0.0)
            st.append(dict(
                p=p, t=eye + p,
                attn16=(kq[CHUNK:2 * CHUNK] * decay).astype(BF16),
                rhs16=jnp.concatenate([v * beta_h, kb * egc], axis=-1).astype(BF16),
                qd16=(q * egc).astype(BF16),
                ktT16=(k * ktf[b][:, col]).T.astype(BF16)))

        for _ in range(5):
            for s in st:
                p16 = s["p"].astype(BF16)
                s["p"] = _dot(p16, p16)
            for s in st:
                s["t"] = s["t"] + _dot(s["t"].astype(BF16), s["p"].astype(BF16))
        for s in st:
            s["sol"] = _dot(s["t"].astype(BF16), s["rhs16"])

        for s, (b, h) in zip(st, grp):
            state = s_scr[b, h]
            s["state"] = state
            lhs = jnp.concatenate([s["sol"][:, HEAD_DIM:].astype(BF16), s["qd16"]], axis=0)
            s["ws"] = _dot(lhs, state.astype(BF16))
        for s, (b, h) in zip(st, grp):
            col = slice(G_LANE + h, G_LANE + h + 1)
            v_new = s["sol"][:, :HEAD_DIM] - s["ws"][0:CHUNK]
            lhs = jnp.concatenate([s["attn16"], s["ktT16"]], axis=0)
            res = _dot(lhs, v_new.astype(BF16))
            s_scr[b, h] = s["state"] * dl[b][:, col] + res[CHUNK:]
            o = s["ws"][CHUNK:] + res[0:CHUNK]
            o = o * lax.rsqrt(jnp.mean(o * o, axis=-1, keepdims=True) + RMS_EPS) * dnw_ref[...]
            z = main_ref[b, :, C_DNZ + h * LANES:C_DNZ + (h + 1) * LANES].astype(F32)
            ocat_ref[b, :, h * LANES:(h + 1) * LANES] = (o * _silu(z)).astype(ocat_ref.dtype)

        rt = []
        for b, h in grp:
            hs = slice(h * LANES, (h + 1) * LANES)
            dmask, xi, zeta, _ = ret_consts[h]
            rq = rq_scr[b, :, hs]
            rk = rk_scr[b, :, hs]
            rstate = r_scr[b, h]
            rt.append(dict(
                scores=_dot_nt(rq.astype(BF16), rk.astype(BF16)) * dmask,
                cross=_dot((rq * xi).astype(BF16), rstate.astype(BF16)),
                rkzT16=(rk * zeta).T.astype(BF16), rstate=rstate))
        for s, (b, h) in zip(rt, grp):
            hs = slice(h * LANES, (h + 1) * LANES)
            rv16 = main_ref[b, :, C_RV + h * LANES:C_RV + (h + 1) * LANES].astype(BF16)
            lhs = jnp.concatenate([s["scores"].astype(BF16), s["rkzT16"]], axis=0)
            res = _dot(lhs, rv16)
            r_scr[b, h] = s["rstate"] * ret_consts[h][3] + res[CHUNK:]
            o_r = res[0:CHUNK] + s["cross"]
            mu = jnp.mean(o_r, axis=-1, keepdims=True)
            dev = o_r - mu
            var = jnp.mean(dev * dev, axis=-1, keepdims=True)
            o_r = dev * lax.rsqrt(var + GN_EPS) * retw_ref[0:1, hs]
            gr = main_ref[b, :, C_RG + h * LANES:C_RG + (h + 1) * LANES].astype(F32)
            ocat_ref[b, :, 2 * WIDTH + h * LANES:2 * WIDTH + (h + 1) * LANES] = (
                o_r * _silu(gr)).astype(ocat_ref.dtype)


def _mix(proj, small, cos, sin, conv, alog, dtb, dnw, retw, poolw, pscale, *, batch, seq, group):
    ts = CHUNK
    nt = seq // ts
    proj3 = proj.reshape(batch, seq, proj.shape[-1])
    small3 = small.reshape(batch, seq, LANES)
    row = lambda t: (0, t, 0)
    const2 = lambda t: (0, 0)
    ocat = pl.pallas_call(
        functools.partial(_mix_body, group=group),
        grid=(nt,),
        in_specs=[
            pl.BlockSpec((batch, ts, C_GATES), row),
            pl.BlockSpec((batch, ts, LANES), row),
            pl.BlockSpec((ts, LANES), lambda t: (t, 0)),
            pl.BlockSpec((ts, LANES), lambda t: (t, 0)),
            pl.BlockSpec((DN_CONV, DN_QKV), const2),
            pl.BlockSpec((1, LANES), const2),
            pl.BlockSpec((1, LANES), const2),
            pl.BlockSpec((1, LANES), const2),
            pl.BlockSpec((1, WIDTH), const2),
            pl.BlockSpec((N_HEADS, LANES, LANES), lambda t: (0, 0, 0)),
            pl.BlockSpec((1, WIDTH), const2),
        ],
        out_specs=pl.BlockSpec((batch, ts, 3 * WIDTH), row),
        out_shape=jax.ShapeDtypeStruct((batch, seq, 3 * WIDTH), BF16),
        scratch_shapes=[
            pltpu.VMEM((batch, N_HEADS, HEAD_DIM, HEAD_DIM), F32),
            pltpu.VMEM((batch, N_HEADS, HEAD_DIM, HEAD_DIM), F32),
            pltpu.VMEM((batch, CONV_HIST, DN_QKV), F32),
            pltpu.VMEM((batch, POOL_HIST, WIDTH), F32),
            pltpu.VMEM((batch, CONV_HIST + ts, DN_QKV), F32),
            pltpu.VMEM((batch, POOL_HIST + ts, WIDTH), F32),
            pltpu.VMEM((batch, ts, DN_QKV), F32),
            pltpu.VMEM((batch, ts, WIDTH), F32),
            pltpu.VMEM((batch, ts, WIDTH), F32),
        ],
        compiler_params=pltpu.CompilerParams(
            dimension_semantics=("arbitrary",), vmem_limit_bytes=VMEM_LIMIT),
        name="mix",
    )(proj3, small3, cos, sin, conv, alog, dtb, dnw, retw, poolw, pscale)
    return ocat.reshape(batch * seq, 3 * WIDTH)


def _merge_body(x_ref, ocat_ref, ga_ref, gb_ref, wdn_ref, wpool_ref, wret_ref, wout_ref, pw_ref, out_ref):
    oc = ocat_ref[...]
    d = _dot(oc[:, 0:WIDTH], wdn_ref[...])
    p = _dot(oc[:, WIDTH:2 * WIDTH], wpool_ref[...])
    r = _dot(oc[:, 2 * WIDTH:3 * WIDTH], wret_ref[...])
    ga = jax.nn.sigmoid(ga_ref[...].astype(F32))
    gb = jax.nn.sigmoid(gb_ref[...].astype(F32))
    half = D_MODEL // 2
    y_lo = (ga[:, 0:half] * d[:, 0:half] + ga[:, D_MODEL:D_MODEL + half] * p[:, 0:half]
            + gb[:, half:D_MODEL] * r[:, 0:half])
    y_hi = (ga[:, half:D_MODEL] * d[:, half:] + gb[:, 0:half] * p[:, half:]
            + gb[:, D_MODEL:D_MODEL + half] * r[:, half:])
    y = jnp.concatenate([y_lo, y_hi], axis=-1).astype(BF16)
    m = _dot(y, wout_ref[...])
    out_ref[...] = x_ref[...] + _rms(m, pw_ref[...])


def _merge(x, ocat, proj, wdn, wpool, wret, wout, pw, *, tm):
    t, d = x.shape
    gw = 3 * D_MODEL // 2
    gblk = C_GATES // gw
    const2 = lambda i: (0, 0)
    return pl.pallas_call(
        _merge_body,
        grid=(t // tm,),
        in_specs=[
            pl.BlockSpec((tm, d), lambda i: (i, 0)),
            pl.BlockSpec((tm, 3 * WIDTH), lambda i: (i, 0)),
            pl.BlockSpec((tm, gw), lambda i: (i, gblk)),
            pl.BlockSpec((tm, gw), lambda i: (i, gblk + 1)),
            pl.BlockSpec((WIDTH, d), const2),
            pl.BlockSpec((WIDTH, d), const2),
            pl.BlockSpec((WIDTH, d), const2),
            pl.BlockSpec((d, d), const2),
            pl.BlockSpec((1, d), const2),
        ],
        out_specs=pl.BlockSpec((tm, d), lambda i: (i, 0)),
        out_shape=jax.ShapeDtypeStruct((t, d), F32),
        compiler_params=pltpu.CompilerParams(
            dimension_semantics=("arbitrary",), vmem_limit_bytes=VMEM_LIMIT),
        name="merge",
    )(x, ocat, proj, proj, wdn, wpool, wret, wout, pw)


def _ffn_body(x_ref, prew_ref, wg_ref, wu_ref, wd_ref, postw_ref, out_ref, h_scr, acc_scr):
    j = pl.program_id(1)

    @pl.when(j == 0)
    def _():
        h_scr[...] = _rms(x_ref[...], prew_ref[...]).astype(BF16)
        acc_scr[...] = jnp.zeros_like(acc_scr)

    h = h_scr[...]
    a = (_silu(_dot(h, wg_ref[...])) * _dot(h, wu_ref[...])).astype(BF16)
    acc_scr[...] += _dot(a, wd_ref[...])

    @pl.when(j == pl.num_programs(1) - 1)
    def _():
        out_ref[...] = x_ref[...] + _rms(acc_scr[...], postw_ref[...])


def _ffn(x, prew, wg, wu, wd, postw, *, tm, tf):
    t, d = x.shape
    f = wg.shape[1]
    return pl.pallas_call(
        _ffn_body,
        grid=(t // tm, f // tf),
        in_specs=[
            pl.BlockSpec((tm, d), lambda i, j: (i, 0)),
            pl.BlockSpec((1, d), lambda i, j: (0, 0)),
            pl.BlockSpec((d, tf), lambda i, j: (0, j)),
            pl.BlockSpec((d, tf), lambda i, j: (0, j)),
            pl.BlockSpec((tf, d), lambda i, j: (j, 0)),
            pl.BlockSpec((1, d), lambda i, j: (0, 0)),
        ],
        out_specs=pl.BlockSpec((tm, d), lambda i, j: (i, 0)),
        out_shape=jax.ShapeDtypeStruct((t, d), F32),
        scratch_shapes=[pltpu.VMEM((tm, d), BF16), pltpu.VMEM((tm, d), F32)],
        compiler_params=pltpu.CompilerParams(
            dimension_semantics=("arbitrary", "arbitrary"), vmem_limit_bytes=VMEM_LIMIT),
        name="ffn",
    )(x, prew, wg, wu, wd, postw)


def _rope_tables(seq):
    half = HEAD_DIM // 2
    inv = ROPE_BASE ** (-jnp.arange(half, dtype=F32) / half)
    ang = jnp.arange(seq, dtype=F32)[:, None] * inv[None, :]
    cos = jnp.cos(ang)
    sin = jnp.sin(ang)
    return jnp.concatenate([cos, cos], axis=-1), jnp.concatenate([-sin, sin], axis=-1)


def _lane_vec(vals):
    return jnp.zeros((1, LANES), F32).at[0, G_LANE:G_LANE + N_HEADS].set(vals.astype(F32))


def kernel(x, mix_pre_norm, mix_post_norm, w_in, dn_conv, dn_A_log, dn_dt_bias, dn_out_norm, ret_out_norm, pool_w, pool_scale, w_branch_dn, w_branch_ret, w_branch_pool, w_out, ffn_pre_norm, ffn_post_norm, ffn_gate, ffn_up, ffn_down):
    batch, seq, d = x.shape
    depth = w_in.shape[0]
    t = batch * seq
    cos, sin = _rope_tables(seq)
    xt = x.reshape(t, d)
    small_lo = C_DNZ + WIDTH
    small_hi = small_lo + 2 * N_HEADS
    for l in range(depth):
        wl = w_in[l]
        w_main = jnp.concatenate([wl[:, :small_lo], wl[:, small_hi:]], axis=1).astype(BF16)
        w_small = jnp.pad(wl[:, small_lo:small_hi], ((0, 0), (0, LANES - 2 * N_HEADS))).astype(BF16)
        proj, small = _proj(xt, mix_pre_norm[l][None, :], w_main, w_small,
                            tm=1024, tn=768, out_dtype=F32)
        ocat = _mix(proj, small, cos, sin, dn_conv[l], _lane_vec(dn_A_log[l]),
                    _lane_vec(dn_dt_bias[l]), dn_out_norm[l][None, :], ret_out_norm[l][None, :],
                    pool_w[l].astype(BF16), pool_scale[l][None, :], batch=batch, seq=seq, group=16)
        xt = _merge(xt, ocat, proj, w_branch_dn[l].astype(BF16), w_branch_pool[l].astype(BF16),
                    w_branch_ret[l].astype(BF16), w_out[l].astype(BF16), mix_post_norm[l][None, :],
                    tm=512)
        xt = _ffn(xt, ffn_pre_norm[l][None, :], ffn_gate[l].astype(BF16), ffn_up[l].astype(BF16),
                  ffn_down[l].astype(BF16), ffn_post_norm[l][None, :], tm=1024, tf=256)
    return xt.reshape(batch, seq, d)
```

```python
import functools
import math

import jax
import jax.numpy as jnp
from jax import lax
from jax.experimental import pallas as pl
from jax.experimental.pallas import tpu as pltpu

F32 = jnp.float32
BF16 = jnp.bfloat16

D_MODEL = 1024
CHUNK = 64
RMS_EPS = 1e-6
GN_EPS = 1e-5
L2_EPS = 1e-6
N_HEADS = 4
HEAD_DIM = 128
DN_QKV = 3 * N_HEADS * HEAD_DIM
WIDTH = N_HEADS * HEAD_DIM
DN_CONV = 4
ROPE_BASE = 10000.0
POOL_WINDOWS = (2, 4, 8, 16)
LANES = 128
CONV_HIST = 8
POOL_HIST = 16

C_QKV = 0
C_DNZ = C_QKV + DN_QKV
C_RQ = C_DNZ + WIDTH
C_RK = C_RQ + WIDTH
C_RV = C_RK + WIDTH
C_RG = C_RV + WIDTH
C_PU = C_RG + WIDTH
C_GATES = C_PU + WIDTH
N_MAIN = C_GATES + 3 * D_MODEL
G_LANE = N_HEADS

VMEM_LIMIT = 56 * 1024 * 1024


def _dot(a, b):
    return jnp.dot(a, b, preferred_element_type=F32)


def _dot_nt(a, b):
    return lax.dot_general(a, b, (((1,), (1,)), ((), ())), preferred_element_type=F32)


def _rms(xf, w):
    return xf * lax.rsqrt(jnp.mean(xf * xf, axis=-1, keepdims=True) + RMS_EPS) * w


def _silu(x):
    return x * jax.nn.sigmoid(x)


def _proj_body(x_ref, nw_ref, w_ref, ws_ref, out_ref, small_ref, h_scr):
    @pl.when(pl.program_id(1) == 0)
    def _():
        h = _rms(x_ref[...], nw_ref[...]).astype(BF16)
        h_scr[...] = h
        small_ref[...] = _dot(h, ws_ref[...])

    out_ref[...] = _dot(h_scr[...], w_ref[...]).astype(out_ref.dtype)


def _proj(x, nw, w_main, w_small, *, tm, tn, out_dtype):
    t, d = x.shape
    n = w_main.shape[1]
    return pl.pallas_call(
        _proj_body,
        grid=(t // tm, n // tn),
        in_specs=[
            pl.BlockSpec((tm, d), lambda i, j: (i, 0)),
            pl.BlockSpec((1, d), lambda i, j: (0, 0)),
            pl.BlockSpec((d, tn), lambda i, j: (0, j)),
            pl.BlockSpec((d, LANES), lambda i, j: (0, 0)),
        ],
        out_specs=[
            pl.BlockSpec((tm, tn), lambda i, j: (i, j)),
            pl.BlockSpec((tm, LANES), lambda i, j: (i, 0)),
        ],
        out_shape=[
            jax.ShapeDtypeStruct((t, n), out_dtype),
            jax.ShapeDtypeStruct((t, LANES), F32),
        ],
        scratch_shapes=[pltpu.VMEM((tm, d), BF16)],
        compiler_params=pltpu.CompilerParams(
            dimension_semantics=("arbitrary", "arbitrary"), vmem_limit_bytes=VMEM_LIMIT),
        name="proj",
    )(x, nw, w_main, w_small)


def _merge_rows(x, oc, ga, gb, wdn_ref, wpool_ref, wret_ref, wout_ref, pw_ref):
    d = _dot(oc[:, 0:WIDTH], wdn_ref[...])
    p = _dot(oc[:, WIDTH:2 * WIDTH], wpool_ref[...])
    r = _dot(oc[:, 2 * WIDTH:3 * WIDTH], wret_ref[...])
    ga = jax.nn.sigmoid(ga.astype(F32))
    gb = jax.nn.sigmoid(gb.astype(F32))
    half = D_MODEL // 2
    y_lo = (ga[:, 0:half] * d[:, 0:half] + ga[:, D_MODEL:D_MODEL + half] * p[:, 0:half]
            + gb[:, half:D_MODEL] * r[:, 0:half])
    y_hi = (ga[:, half:D_MODEL] * d[:, half:] + gb[:, 0:half] * p[:, half:]
            + gb[:, D_MODEL:D_MODEL + half] * r[:, half:])
    y = jnp.concatenate([y_lo, y_hi], axis=-1).astype(BF16)
    return x + _rms(_dot(y, wout_ref[...]), pw_ref[...])


def _mix_body(main_ref, small_ref, cos_ref, sin_ref, conv_ref, alog_ref, dtb_ref, dnw_ref,
              retw_ref, poolw_ref, pscale_ref, x_ref, ga_ref, gb_ref, wdn_ref, wpool_ref,
              wret_ref, wout_ref, pw_ref, out_ref,
              s_scr, r_scr, chist, phist, cbuf, pbuf, qkv_scr, rq_scr, rk_scr, ocat_scr,
              *, group, merge_rows):
    t_idx = pl.program_id(0)
    nb, ts = main_ref.shape[0], main_ref.shape[1]
    assert ts == CHUNK
    slot_w = lax.rem(t_idx, 2)
    slot_r = 1 - slot_w

    @pl.when(t_idx == 0)
    def _():
        s_scr[...] = jnp.zeros_like(s_scr)
        r_scr[...] = jnp.zeros_like(r_scr)
        chist[...] = jnp.zeros_like(chist)
        phist[...] = jnp.zeros_like(phist)
        ocat_scr[...] = jnp.zeros_like(ocat_scr)

    for b0 in range(0, nb, merge_rows):
        bs = slice(b0, b0 + merge_rows)
        rows = merge_rows * ts
        res = _merge_rows(
            x_ref[bs].reshape(rows, D_MODEL),
            ocat_scr[slot_r, bs].reshape(rows, 3 * WIDTH),
            ga_ref[bs].reshape(rows, ga_ref.shape[-1]),
            gb_ref[bs].reshape(rows, gb_ref.shape[-1]),
            wdn_ref, wpool_ref, wret_ref, wout_ref, pw_ref)
        out_ref[bs] = res.reshape(merge_rows, ts, D_MODEL)

    ri = lax.broadcasted_iota(jnp.int32, (CHUNK, CHUNK), 0)
    ci = lax.broadcasted_iota(jnp.int32, (CHUNK, CHUNK), 1)
    causal = ri >= ci
    strict = ri > ci
    eye = jnp.where(ri == ci, 1.0, 0.0).astype(F32)
    tril = jnp.where(causal, 1.0, 0.0).astype(F32)
    absd = jnp.abs(ri - ci).astype(F32)
    rowf = lax.broadcasted_iota(jnp.int32, (CHUNK, LANES), 0).astype(F32)
    chunk_idx = jnp.minimum(t_idx, pl.num_programs(0) - 2)
    pos = rowf + (chunk_idx * ts + 1).astype(F32)
    inv_cnt = [1.0 / jnp.minimum(pos, float(win)) for win in POOL_WINDOWS]
    cos = cos_ref[...]
    sin = sin_ref[...]

    g_rows = []
    beta = []
    for b in range(nb):
        cbuf[b, 0:CONV_HIST, :] = chist[b]
        cbuf[b, CONV_HIST:CONV_HIST + ts, :] = main_ref[b, :, C_QKV:C_QKV + DN_QKV].astype(F32)
        chist[b] = cbuf[b, ts:ts + CONV_HIST, :]
        base = CONV_HIST - (DN_CONV - 1)
        for j in range(DN_QKV // LANES):
            cs = slice(j * LANES, (j + 1) * LANES)
            acc = conv_ref[0:1, cs] * cbuf[b, base:base + ts, cs]
            for k in range(1, DN_CONV):
                acc = acc + conv_ref[k:k + 1, cs] * cbuf[b, base + k:base + k + ts, cs]
            y = _silu(acc)
            if j < 2 * N_HEADS:
                y = y * lax.rsqrt(jnp.sum(y * y, axis=-1, keepdims=True) + L2_EPS)
            if j < N_HEADS:
                y = y * (HEAD_DIM ** -0.5)
            qkv_scr[b, :, cs] = y

        for h in range(N_HEADS):
            cs = slice(h * LANES, (h + 1) * LANES)
            xq = main_ref[b, :, C_RQ + h * LANES:C_RQ + (h + 1) * LANES].astype(F32)
            rq_scr[b, :, cs] = xq * cos + pltpu.roll(xq, HEAD_DIM // 2, 1) * sin
            xk = main_ref[b, :, C_RK + h * LANES:C_RK + (h + 1) * LANES].astype(F32)
            rk_scr[b, :, cs] = (xk * cos + pltpu.roll(xk, HEAD_DIM // 2, 1) * sin) * (HEAD_DIM ** -0.5)

        pbuf[b, 0:POOL_HIST, :] = phist[b]
        pbuf[b, POOL_HIST:POOL_HIST + ts, :] = main_ref[b, :, C_PU:C_PU + WIDTH].astype(F32)
        phist[b] = pbuf[b, ts:ts + POOL_HIST, :]
        for gi in range(len(POOL_WINDOWS)):
            cs = slice(gi * LANES, (gi + 1) * LANES)
            wsum = pbuf[b, :, cs]
            for lvl in range(gi + 1):
                wsum = wsum + pltpu.roll(wsum, 1 << lvl, 0)
            mixed = wsum[POOL_HIST:] * inv_cnt[gi] - pbuf[b, POOL_HIST:POOL_HIST + ts, cs]
            y = _dot(mixed.astype(BF16), poolw_ref[gi]) * pscale_ref[0:1, cs]
            ocat_scr[slot_w, b, :, WIDTH + gi * LANES:WIDTH + (gi + 1) * LANES] = y.astype(BF16)

        sm = small_ref[b]
        beta.append(jax.nn.sigmoid(sm))
        g_rows.append(-(jnp.exp(alog_ref[...]) * jax.nn.softplus(sm + dtb_ref[...])))

    gcol_all = jnp.dot(tril, jnp.concatenate(g_rows, axis=-1),
                       precision=lax.Precision.HIGHEST, preferred_element_type=F32)
    gcol, grow, eg, ktf, dl = [], [], [], [], []
    for b in range(nb):
        gc = gcol_all[:, b * LANES:(b + 1) * LANES]
        gl = gc[CHUNK - 1:CHUNK, :]
        gcol.append(gc)
        grow.append(gc.T[0:2 * N_HEADS, :])
        eg.append(jnp.exp(gc))
        ktf.append(jnp.exp(gl - gc))
        dl.append(jnp.exp(gl))

    ret_consts = []
    for h in range(N_HEADS):
        lg = math.log1p(-(2.0 ** (-5.0 - h)))
        ret_consts.append((jnp.exp(absd * lg), jnp.exp((rowf + 1.0) * lg),
                           jnp.exp((CHUNK - 1.0 - rowf) * lg), math.exp(CHUNK * lg)))

    streams = [(b, h) for b in range(nb) for h in range(N_HEADS)]
    for g0 in range(0, len(streams), group):
        grp = streams[g0:g0 + group]

        st = []
        for b, h in grp:
            col = slice(G_LANE + h, G_LANE + h + 1)
            q = qkv_scr[b, :, h * LANES:(h + 1) * LANES]
            k = qkv_scr[b, :, WIDTH + h * LANES:WIDTH + (h + 1) * LANES]
            v = qkv_scr[b, :, 2 * WIDTH + h * LANES:2 * WIDTH + (h + 1) * LANES]
            diff = gcol[b][:, col] - grow[b][G_LANE + h:G_LANE + h + 1, :]
            decay = jnp.where(causal, jnp.exp(jnp.where(causal, diff, 0.0)), 0.0)
            beta_h = beta[b][:, h:h + 1]
            kb = k * beta_h
            egc = eg[b][:, col]
            kq = _dot_nt(jnp.concatenate([kb, q], axis=0).astype(BF16), k.astype(BF16))
            p = -jnp.where(strict, kq[0:CHUNK] * decay, 0.0)
            st.append(dict(
                p=p, t=eye + p,
                attn16=(kq[CHUNK:2 * CHUNK] * decay).astype(BF16),
                rhs16=jnp.concatenate([v * beta_h, kb * egc], axis=-1).astype(BF16),
                qd16=(q * egc).astype(BF16),
                ktT16=(k * ktf[b][:, col]).T.astype(BF16)))

        for _ in range(5):
            for s in st:
                p16 = s["p"].astype(BF16)
                s["p"] = _dot(p16, p16)
            for s in st:
                s["t"] = s["t"] + _dot(s["t"].astype(BF16), s["p"].astype(BF16))
        for s in st:
            s["sol"] = _dot(s["t"].astype(BF16), s["rhs16"])

        for s, (b, h) in zip(st, grp):
            state = s_scr[b, h]
            s["state"] = state
            lhs = jnp.concatenate([s["sol"][:, HEAD_DIM:].astype(BF16), s["qd16"]], axis=0)
            s["ws"] = _dot(lhs, state.astype(BF16))
        for s, (b, h) in zip(st, grp):
            col = slice(G_LANE + h, G_LANE + h + 1)
            v_new = s["sol"][:, :HEAD_DIM] - s["ws"][0:CHUNK]
            lhs = jnp.concatenate([s["attn16"], s["ktT16"]], axis=0)
            res = _dot(lhs, v_new.astype(BF16))
            s_scr[b, h] = s["state"] * dl[b][:, col] + res[CHUNK:]
            o = s["ws"][CHUNK:] + res[0:CHUNK]
            o = o * lax.rsqrt(jnp.mean(o * o, axis=-1, keepdims=True) + RMS_EPS) * dnw_ref[...]
            z = main_ref[b, :, C_DNZ + h * LANES:C_DNZ + (h + 1) * LANES].astype(F32)
            ocat_scr[slot_w, b, :, h * LANES:(h + 1) * LANES] = (o * _silu(z)).astype(BF16)

        rt = []
        for b, h in grp:
            hs = slice(h * LANES, (h + 1) * LANES)
            dmask, xi, zeta, _ = ret_consts[h]
            rq = rq_scr[b, :, hs]
            rk = rk_scr[b, :, hs]
            rstate = r_scr[b, h]
            rt.append(dict(
                scores=_dot_nt(rq.astype(BF16), rk.astype(BF16)) * dmask,
                cross=_dot((rq * xi).astype(BF16), rstate.astype(BF16)),
                rkzT16=(rk * zeta).T.astype(BF16), rstate=rstate))
        for s, (b, h) in zip(rt, grp):
            hs = slice(h * LANES, (h + 1) * LANES)
            rv16 = main_ref[b, :, C_RV + h * LANES:C_RV + (h + 1) * LANES].astype(BF16)
            lhs = jnp.concatenate([s["scores"].astype(BF16), s["rkzT16"]], axis=0)
            res = _dot(lhs, rv16)
            r_scr[b, h] = s["rstate"] * ret_consts[h][3] + res[CHUNK:]
            o_r = res[0:CHUNK] + s["cross"]
            mu = jnp.mean(o_r, axis=-1, keepdims=True)
            dev = o_r - mu
            var = jnp.mean(dev * dev, axis=-1, keepdims=True)
            o_r = dev * lax.rsqrt(var + GN_EPS) * retw_ref[0:1, hs]
            gr = main_ref[b, :, C_RG + h * LANES:C_RG + (h + 1) * LANES].astype(F32)
            ocat_scr[slot_w, b, :, 2 * WIDTH + h * LANES:2 * WIDTH + (h + 1) * LANES] = (
                o_r * _silu(gr)).astype(BF16)


def _mix(x, proj, small, cos, sin, conv, alog, dtb, dnw, retw, poolw, pscale, wdn, wpool, wret,
         wout, pw, *, batch, seq, group, merge_rows):
    ts = CHUNK
    nt = seq // ts
    d = x.shape[-1]
    x3 = x.reshape(batch, seq, d)
    proj3 = proj.reshape(batch, seq, proj.shape[-1])
    small3 = small.reshape(batch, seq, LANES)
    gw = 3 * D_MODEL // 2
    gblk = C_GATES // gw
    cur = lambda t: (0, jnp.minimum(t, nt - 1), 0)
    prev = lambda t: (0, jnp.maximum(t - 1, 0), 0)
    const2 = lambda t: (0, 0)
    resident = dict(pipeline_mode=pl.Buffered(1))
    out = pl.pallas_call(
        functools.partial(_mix_body, group=group, merge_rows=merge_rows),
        grid=(nt + 1,),
        in_specs=[
            pl.BlockSpec((batch, ts, C_GATES), cur),
            pl.BlockSpec((batch, ts, LANES), cur),
            pl.BlockSpec((ts, LANES), lambda t: (jnp.minimum(t, nt - 1), 0)),
            pl.BlockSpec((ts, LANES), lambda t: (jnp.minimum(t, nt - 1), 0)),
            pl.BlockSpec((DN_CONV, DN_QKV), const2),
            pl.BlockSpec((1, LANES), const2),
            pl.BlockSpec((1, LANES), const2),
            pl.BlockSpec((1, LANES), const2),
            pl.BlockSpec((1, WIDTH), const2),
            pl.BlockSpec((N_HEADS, LANES, LANES), lambda t: (0, 0, 0)),
            pl.BlockSpec((1, WIDTH), const2),
            pl.BlockSpec((batch, ts, d), prev),
            pl.BlockSpec((batch, ts, gw), lambda t: (0, jnp.maximum(t - 1, 0), gblk)),
            pl.BlockSpec((batch, ts, gw), lambda t: (0, jnp.maximum(t - 1, 0), gblk + 1)),
            pl.BlockSpec((WIDTH, d), const2, **resident),
            pl.BlockSpec((WIDTH, d), const2, **resident),
            pl.BlockSpec((WIDTH, d), const2, **resident),
            pl.BlockSpec((d, d), const2, **resident),
            pl.BlockSpec((1, d), const2),
        ],
        out_specs=pl.BlockSpec((batch, ts, d), prev),
        out_shape=jax.ShapeDtypeStruct((batch, seq, d), F32),
        scratch_shapes=[
            pltpu.VMEM((batch, N_HEADS, HEAD_DIM, HEAD_DIM), F32),
            pltpu.VMEM((batch, N_HEADS, HEAD_DIM, HEAD_DIM), F32),
            pltpu.VMEM((batch, CONV_HIST, DN_QKV), F32),
            pltpu.VMEM((batch, POOL_HIST, WIDTH), F32),
            pltpu.VMEM((batch, CONV_HIST + ts, DN_QKV), F32),
            pltpu.VMEM((batch, POOL_HIST + ts, WIDTH), F32),
            pltpu.VMEM((batch, ts, DN_QKV), F32),
            pltpu.VMEM((batch, ts, WIDTH), F32),
            pltpu.VMEM((batch, ts, WIDTH), F32),
            pltpu.VMEM((2, batch, ts, 3 * WIDTH), BF16),
        ],
        compiler_params=pltpu.CompilerParams(
            dimension_semantics=("arbitrary",), vmem_limit_bytes=VMEM_LIMIT),
        name="mix",
    )(proj3, small3, cos, sin, conv, alog, dtb, dnw, retw, poolw, pscale, x3, proj3, proj3,
      wdn, wpool, wret, wout, pw)
    return out.reshape(batch * seq, d)


def _ffn_body(x_ref, prew_ref, wg_ref, wu_ref, wd_ref, postw_ref, out_ref, h_scr, acc_scr):
    j = pl.program_id(1)

    @pl.when(j == 0)
    def _():
        h_scr[...] = _rms(x_ref[...], prew_ref[...]).astype(BF16)
        acc_scr[...] = jnp.zeros_like(acc_scr)

    h = h_scr[...]
    a = (_silu(_dot(h, wg_ref[...])) * _dot(h, wu_ref[...])).astype(BF16)
    acc_scr[...] += _dot(a, wd_ref[...])

    @pl.when(j == pl.num_programs(1) - 1)
    def _():
        out_ref[...] = x_ref[...] + _rms(acc_scr[...], postw_ref[...])


def _ffn(x, prew, wg, wu, wd, postw, *, tm, tf):
    t, d = x.shape
    f = wg.shape[1]
    return pl.pallas_call(
        _ffn_body,
        grid=(t // tm, f // tf),
        in_specs=[
            pl.BlockSpec((tm, d), lambda i, j: (i, 0)),
            pl.BlockSpec((1, d), lambda i, j: (0, 0)),
            pl.BlockSpec((d, tf), lambda i, j: (0, j)),
            pl.BlockSpec((d, tf), lambda i, j: (0, j)),
            pl.BlockSpec((tf, d), lambda i, j: (j, 0)),
            pl.BlockSpec((1, d), lambda i, j: (0, 0)),
        ],
        out_specs=pl.BlockSpec((tm, d), lambda i, j: (i, 0)),
        out_shape=jax.ShapeDtypeStruct((t, d), F32),
        scratch_shapes=[pltpu.VMEM((tm, d), BF16), pltpu.VMEM((tm, d), F32)],
        compiler_params=pltpu.CompilerParams(
            dimension_semantics=("arbitrary", "arbitrary"), vmem_limit_bytes=VMEM_LIMIT),
        name="ffn",
    )(x, prew, wg, wu, wd, postw)


def _rope_tables(seq):
    half = HEAD_DIM // 2
    inv = ROPE_BASE ** (-jnp.arange(half, dtype=F32) / half)
    ang = jnp.arange(seq, dtype=F32)[:, None] * inv[None, :]
    cos = jnp.cos(ang)
    sin = jnp.sin(ang)
    return jnp.concatenate([cos, cos], axis=-1), jnp.concatenate([-sin, sin], axis=-1)


def _lane_vec(vals):
    return jnp.zeros((1, LANES), F32).at[0, G_LANE:G_LANE + N_HEADS].set(vals.astype(F32))


def kernel(x, mix_pre_norm, mix_post_norm, w_in, dn_conv, dn_A_log, dn_dt_bias, dn_out_norm, ret_out_norm, pool_w, pool_scale, w_branch_dn, w_branch_ret, w_branch_pool, w_out, ffn_pre_norm, ffn_post_norm, ffn_gate, ffn_up, ffn_down):
    batch, seq, d = x.shape
    depth = w_in.shape[0]
    t = batch * seq
    cos, sin = _rope_tables(seq)
    xt = x.reshape(t, d)
    small_lo = C_DNZ + WIDTH
    small_hi = small_lo + 2 * N_HEADS
    for l in range(depth):
        wl = w_in[l]
        w_main = jnp.concatenate([wl[:, :small_lo], wl[:, small_hi:]], axis=1).astype(BF16)
        w_small = jnp.pad(wl[:, small_lo:small_hi], ((0, 0), (0, LANES - 2 * N_HEADS))).astype(BF16)
        proj, small = _proj(xt, mix_pre_norm[l][None, :], w_main, w_small,
                            tm=1024, tn=768, out_dtype=BF16)
        xt = _mix(xt, proj, small, cos, sin, dn_conv[l], _lane_vec(dn_A_log[l]),
                  _lane_vec(dn_dt_bias[l]), dn_out_norm[l][None, :], ret_out_norm[l][None, :],
                  pool_w[l].astype(BF16), pool_scale[l][None, :], w_branch_dn[l].astype(BF16),
                  w_branch_pool[l].astype(BF16), w_branch_ret[l].astype(BF16), w_out[l].astype(BF16),
                  mix_post_norm[l][None, :], batch=batch, seq=seq, group=16, merge_rows=4)
        xt = _ffn(xt, ffn_pre_norm[l][None, :], ffn_gate[l].astype(BF16), ffn_up[l].astype(BF16),
                  ffn_down[l].astype(BF16), ffn_post_norm[l][None, :], tm=1024, tf=256)
    return xt.reshape(batch, seq, d)
```

```python
import functools
import math

import jax
import jax.numpy as jnp
from jax import lax
from jax.experimental import pallas as pl
from jax.experimental.pallas import tpu as pltpu

F32 = jnp.float32
BF16 = jnp.bfloat16

D_MODEL = 1024
CHUNK = 64
RMS_EPS = 1e-6
GN_EPS = 1e-5
L2_EPS = 1e-6
N_HEADS = 4
HEAD_DIM = 128
DN_QKV = 3 * N_HEADS * HEAD_DIM
WIDTH = N_HEADS * HEAD_DIM
DN_CONV = 4
ROPE_BASE = 10000.0
POOL_WINDOWS = (2, 4, 8, 16)
LANES = 128
CONV_HIST = 8
POOL_HIST = 16

C_QKV = 0
C_DNZ = C_QKV + DN_QKV
C_RQ = C_DNZ + WIDTH
C_RK = C_RQ + WIDTH
C_RV = C_RK + WIDTH
C_RG = C_RV + WIDTH
C_PU = C_RG + WIDTH
C_GATES = C_PU + WIDTH
N_MAIN = C_GATES + 3 * D_MODEL
PROJ_TN = DN_QKV
G_LANE = N_HEADS

VMEM_LIMIT = 56 * 1024 * 1024


def _dot(a, b):
    return jnp.dot(a, b, preferred_element_type=F32)


def _dot_nt(a, b):
    return lax.dot_general(a, b, (((1,), (1,)), ((), ())), preferred_element_type=F32)


def _rms(xf, w):
    return xf * lax.rsqrt(jnp.mean(xf * xf, axis=-1, keepdims=True) + RMS_EPS) * w


def _silu(x):
    return x * jax.nn.sigmoid(x)


def _rotary(x, cos, sin):
    return x * cos + pltpu.roll(x, HEAD_DIM // 2, 1) * sin


def _proj_body(x_ref, nw_ref, w_ref, ws_ref, cos_ref, sin_ref, conv_ref, out_ref, small_ref,
               h_scr, cbuf, hist, *, tiles_per_seq, row_block):
    i = pl.program_id(0)
    j = pl.program_id(1)
    tm = x_ref.shape[0]
    seq_start = lax.rem(i, tiles_per_seq) == 0

    @pl.when((j == 0) & seq_start)
    def _():
        hist[...] = jnp.zeros_like(hist)

    @pl.when(j == 0)
    def _():
        h = _rms(x_ref[...], nw_ref[...]).astype(BF16)
        h_scr[...] = h
        small_ref[...] = _dot(h, ws_ref[...])
        cbuf[0:CONV_HIST, :] = hist[...]
        cbuf[CONV_HIST:CONV_HIST + tm, :] = _dot(h, w_ref[...])
        hist[...] = cbuf[tm:tm + CONV_HIST, :]
        for r0 in range(0, tm, row_block):
            for s in range(DN_QKV // LANES):
                cs = slice(s * LANES, (s + 1) * LANES)
                xb = cbuf[r0:r0 + CONV_HIST + row_block, cs]
                acc = conv_ref[DN_CONV - 1:DN_CONV, cs] * xb
                for k in range(DN_CONV - 1):
                    acc = acc + conv_ref[k:k + 1, cs] * pltpu.roll(xb, DN_CONV - 1 - k, 0)
                y = _silu(acc[CONV_HIST:])
                if s < 2 * N_HEADS:
                    scale = HEAD_DIM ** -0.5 if s < N_HEADS else 1.0
                    y = y * (lax.rsqrt(jnp.sum(y * y, axis=-1, keepdims=True) + L2_EPS) * scale)
                out_ref[r0:r0 + row_block, cs] = y.astype(out_ref.dtype)

    @pl.when(j == 1)
    def _():
        res = _dot(h_scr[...], w_ref[...])
        cos = cos_ref[...]
        sin = sin_ref[...]
        for s in range(PROJ_TN // LANES):
            cs = slice(s * LANES, (s + 1) * LANES)
            v = res[:, cs]
            if s < N_HEADS:
                y = _silu(v)
            elif s < 2 * N_HEADS:
                y = _rotary(v, cos, sin)
            else:
                y = _rotary(v, cos, sin) * (HEAD_DIM ** -0.5)
            out_ref[:, cs] = y.astype(out_ref.dtype)

    @pl.when(j == 2)
    def _():
        res = _dot(h_scr[...], w_ref[...])
        for s in range(PROJ_TN // LANES):
            cs = slice(s * LANES, (s + 1) * LANES)
            v = res[:, cs]
            y = _silu(v) if N_HEADS <= s < 2 * N_HEADS else v
            out_ref[:, cs] = y.astype(out_ref.dtype)

    @pl.when(j >= 3)
    def _():
        out_ref[...] = jax.nn.sigmoid(_dot(h_scr[...], w_ref[...])).astype(out_ref.dtype)


def _proj(x, nw, w_main, w_small, cos, sin, conv, *, layer, seq, tm):
    t, d = x.shape
    tn = PROJ_TN
    n = w_main.shape[-1]
    assert C_DNZ == tn and C_RV == 2 * tn and C_GATES == 3 * tn and n % tn == 0
    tiles_per_seq = seq // tm
    return pl.pallas_call(
        functools.partial(_proj_body, tiles_per_seq=tiles_per_seq, row_block=64),
        grid=(t // tm, n // tn),
        in_specs=[
            pl.BlockSpec((tm, d), lambda i, j: (i, 0)),
            pl.BlockSpec((None, 1, d), lambda i, j: (layer, 0, 0)),
            pl.BlockSpec((None, d, tn), lambda i, j: (layer, 0, j)),
            pl.BlockSpec((None, d, LANES), lambda i, j: (layer, 0, 0)),
            pl.BlockSpec((tm, LANES), lambda i, j: (lax.rem(i, tiles_per_seq), 0)),
            pl.BlockSpec((tm, LANES), lambda i, j: (lax.rem(i, tiles_per_seq), 0)),
            pl.BlockSpec((None, DN_CONV, DN_QKV), lambda i, j: (layer, 0, 0)),
        ],
        out_specs=[
            pl.BlockSpec((tm, tn), lambda i, j: (i, j)),
            pl.BlockSpec((tm, LANES), lambda i, j: (i, 0)),
        ],
        out_shape=[
            jax.ShapeDtypeStruct((t, n), BF16),
            jax.ShapeDtypeStruct((t, LANES), F32),
        ],
        scratch_shapes=[
            pltpu.VMEM((tm, d), BF16),
            pltpu.VMEM((CONV_HIST + tm, DN_QKV), F32),
            pltpu.VMEM((CONV_HIST, DN_QKV), F32),
        ],
        compiler_params=pltpu.CompilerParams(
            dimension_semantics=("arbitrary", "arbitrary"), vmem_limit_bytes=VMEM_LIMIT),
        name="proj",
    )(x, nw, w_main, w_small, cos, sin, conv)


def _merge_rows(x, oc, ga, gb, wdn_ref, wpool_ref, wret_ref, wout_ref, pw_ref):
    d = _dot(oc[:, 0:WIDTH], wdn_ref[...])
    p = _dot(oc[:, WIDTH:2 * WIDTH], wpool_ref[...])
    r = _dot(oc[:, 2 * WIDTH:3 * WIDTH], wret_ref[...])
    ga = ga.astype(F32)
    gb = gb.astype(F32)
    half = D_MODEL // 2
    y_lo = (ga[:, 0:half] * d[:, 0:half] + ga[:, D_MODEL:D_MODEL + half] * p[:, 0:half]
            + gb[:, half:D_MODEL] * r[:, 0:half])
    y_hi = (ga[:, half:D_MODEL] * d[:, half:] + gb[:, 0:half] * p[:, half:]
            + gb[:, D_MODEL:D_MODEL + half] * r[:, half:])
    y = jnp.concatenate([y_lo, y_hi], axis=-1).astype(BF16)
    return x + _rms(_dot(y, wout_ref[...]), pw_ref[...])


def _mix_body(main_ref, small_ref, alog_ref, dtb_ref, dnw_ref,
              retw_ref, poolw_ref, pscale_ref, x_ref, ga_ref, gb_ref, wdn_ref, wpool_ref,
              wret_ref, wout_ref, pw_ref, out_ref,
              s_scr, r_scr, phist, pbuf, ocat_scr, *, group, merge_rows):
    t_idx = pl.program_id(0)
    nb, ts = main_ref.shape[0], main_ref.shape[1]
    assert ts == CHUNK
    slot_w = lax.rem(t_idx, 2)
    slot_r = 1 - slot_w

    @pl.when(t_idx == 0)
    def _():
        s_scr[...] = jnp.zeros_like(s_scr)
        r_scr[...] = jnp.zeros_like(r_scr)
        phist[...] = jnp.zeros_like(phist)
        ocat_scr[...] = jnp.zeros_like(ocat_scr)

    for b0 in range(0, nb, merge_rows):
        bs = slice(b0, b0 + merge_rows)
        rows = merge_rows * ts
        res = _merge_rows(
            x_ref[bs].reshape(rows, D_MODEL),
            ocat_scr[slot_r, bs].reshape(rows, 3 * WIDTH),
            ga_ref[bs].reshape(rows, ga_ref.shape[-1]),
            gb_ref[bs].reshape(rows, gb_ref.shape[-1]),
            wdn_ref, wpool_ref, wret_ref, wout_ref, pw_ref)
        out_ref[bs] = res.reshape(merge_rows, ts, D_MODEL)

    ri = lax.broadcasted_iota(jnp.int32, (CHUNK, CHUNK), 0)
    ci = lax.broadcasted_iota(jnp.int32, (CHUNK, CHUNK), 1)
    causal = ri >= ci
    strict = ri > ci
    eye = jnp.where(ri == ci, 1.0, 0.0).astype(F32)
    tril = jnp.where(causal, 1.0, 0.0).astype(F32)
    absd = jnp.abs(ri - ci).astype(F32)
    rowf = lax.broadcasted_iota(jnp.int32, (CHUNK, LANES), 0).astype(F32)
    chunk_idx = jnp.minimum(t_idx, pl.num_programs(0) - 2)
    pos = rowf + (chunk_idx * ts + 1).astype(F32)
    inv_cnt = [1.0 / jnp.minimum(pos, float(win)) for win in POOL_WINDOWS]

    g_rows = []
    beta = []
    for b in range(nb):
        pbuf[b, 0:POOL_HIST, :] = phist[b]
        pbuf[b, POOL_HIST:POOL_HIST + ts, :] = main_ref[b, :, C_PU:C_PU + WIDTH].astype(F32)
        phist[b] = pbuf[b, ts:ts + POOL_HIST, :]
        for gi in range(len(POOL_WINDOWS)):
            cs = slice(gi * LANES, (gi + 1) * LANES)
            wsum = pbuf[b, :, cs]
            for lvl in range(gi + 1):
                wsum = wsum + pltpu.roll(wsum, 1 << lvl, 0)
            mixed = wsum[POOL_HIST:] * inv_cnt[gi] - pbuf[b, POOL_HIST:POOL_HIST + ts, cs]
            y = _dot(mixed.astype(BF16), poolw_ref[gi]) * pscale_ref[0:1, cs]
            ocat_scr[slot_w, b, :, WIDTH + gi * LANES:WIDTH + (gi + 1) * LANES] = y.astype(BF16)

        sm = small_ref[b]
        beta.append(jax.nn.sigmoid(sm))
        g_rows.append(-(jnp.exp(alog_ref[...]) * jax.nn.softplus(sm + dtb_ref[...])))

    gcol_all = jnp.dot(tril, jnp.concatenate(g_rows, axis=-1),
                       precision=lax.Precision.HIGHEST, preferred_element_type=F32)
    gcol, grow, eg, ktf, dl = [], [], [], [], []
    for b in range(nb):
        gc = gcol_all[:, b * LANES:(b + 1) * LANES]
        gl = gc[CHUNK - 1:CHUNK, :]
        gcol.append(gc)
        grow.append(gc.T[0:2 * N_HEADS, :])
        eg.append(jnp.exp(gc))
        ktf.append(jnp.exp(gl - gc))
        dl.append(jnp.exp(gl))

    ret_consts = []
    for h in range(N_HEADS):
        lg = math.log1p(-(2.0 ** (-5.0 - h)))
        ret_consts.append((jnp.exp(absd * lg), jnp.exp((rowf + 1.0) * lg),
                           jnp.exp((CHUNK - 1.0 - rowf) * lg), math.exp(CHUNK * lg)))

    streams = [(b, h) for b in range(nb) for h in range(N_HEADS)]
    for g0 in range(0, len(streams), group):
        grp = streams[g0:g0 + group]

        st = []
        for b, h in grp:
            col = slice(G_LANE + h, G_LANE + h + 1)
            q16 = main_ref[b, :, h * LANES:(h + 1) * LANES]
            k16 = main_ref[b, :, WIDTH + h * LANES:WIDTH + (h + 1) * LANES]
            q = q16.astype(F32)
            k = k16.astype(F32)
            v = main_ref[b, :, 2 * WIDTH + h * LANES:2 * WIDTH + (h + 1) * LANES].astype(F32)
            diff = gcol[b][:, col] - grow[b][G_LANE + h:G_LANE + h + 1, :]
            decay = jnp.where(causal, jnp.exp(jnp.where(causal, diff, 0.0)), 0.0)
            beta_h = beta[b][:, h:h + 1]
            kb = k * beta_h
            egc = eg[b][:, col]
            kq = _dot_nt(jnp.concatenate([kb.astype(BF16), q16], axis=0), k16)
            p = -jnp.where(strict, kq[0:CHUNK] * decay, 0.0)
            st.append(dict(
                p=p, t=eye + p,
                attn16=(kq[CHUNK:2 * CHUNK] * decay).astype(BF16),
                rhs16=jnp.concatenate([v * beta_h, kb * egc], axis=-1).astype(BF16),
                qd16=(q * egc).astype(BF16),
                ktT16=(k * ktf[b][:, col]).T.astype(BF16)))

        for _ in range(5):
            for s in st:
                p16 = s["p"].astype(BF16)
                s["p"] = _dot(p16, p16)
            for s in st:
                s["t"] = s["t"] + _dot(s["t"].astype(BF16), s["p"].astype(BF16))
        for s in st:
            s["sol"] = _dot(s["t"].astype(BF16), s["rhs16"])

        for s, (b, h) in zip(st, grp):
            state = s_scr[b, h]
            s["state"] = state
            lhs = jnp.concatenate([s["sol"][:, HEAD_DIM:].astype(BF16), s["qd16"]], axis=0)
            s["ws"] = _dot(lhs, state.astype(BF16))
        for s, (b, h) in zip(st, grp):
            col = slice(G_LANE + h, G_LANE + h + 1)
            v_new = s["sol"][:, :HEAD_DIM] - s["ws"][0:CHUNK]
            lhs = jnp.concatenate([s["attn16"], s["ktT16"]], axis=0)
            res = _dot(lhs, v_new.astype(BF16))
            s_scr[b, h] = s["state"] * dl[b][:, col] + res[CHUNK:]
            o = s["ws"][CHUNK:] + res[0:CHUNK]
            o = o * lax.rsqrt(jnp.mean(o * o, axis=-1, keepdims=True) + RMS_EPS) * dnw_ref[...]
            z = main_ref[b, :, C_DNZ + h * LANES:C_DNZ + (h + 1) * LANES].astype(F32)
            ocat_scr[slot_w, b, :, h * LANES:(h + 1) * LANES] = (o * z).astype(BF16)

        rt = []
        for b, h in grp:
            dmask, xi, zeta, _ = ret_consts[h]
            rq16 = main_ref[b, :, C_RQ + h * LANES:C_RQ + (h + 1) * LANES]
            rk16 = main_ref[b, :, C_RK + h * LANES:C_RK + (h + 1) * LANES]
            rstate = r_scr[b, h]
            rt.append(dict(
                scores=_dot_nt(rq16, rk16) * dmask,
                cross=_dot((rq16.astype(F32) * xi).astype(BF16), rstate.astype(BF16)),
                rkzT16=(rk16.astype(F32) * zeta).T.astype(BF16), rstate=rstate))
        for s, (b, h) in zip(rt, grp):
            hs = slice(h * LANES, (h + 1) * LANES)
            rv16 = main_ref[b, :, C_RV + h * LANES:C_RV + (h + 1) * LANES]
            lhs = jnp.concatenate([s["scores"].astype(BF16), s["rkzT16"]], axis=0)
            res = _dot(lhs, rv16)
            r_scr[b, h] = s["rstate"] * ret_consts[h][3] + res[CHUNK:]
            o_r = res[0:CHUNK] + s["cross"]
            mu = jnp.mean(o_r, axis=-1, keepdims=True)
            dev = o_r - mu
            var = jnp.mean(dev * dev, axis=-1, keepdims=True)
            o_r = dev * lax.rsqrt(var + GN_EPS) * retw_ref[0:1, hs]
            gr = main_ref[b, :, C_RG + h * LANES:C_RG + (h + 1) * LANES].astype(F32)
            ocat_scr[slot_w, b, :, 2 * WIDTH + h * LANES:2 * WIDTH + (h + 1) * LANES] = (
                o_r * gr).astype(BF16)


def _mix(x, proj, small, alog, dtb, dnw, retw, poolw, pscale, wdn, wpool, wret, wout, pw,
         *, layer, batch, seq, group, merge_rows):
    ts = CHUNK
    nt = seq // ts
    d = x.shape[-1]
    x3 = x.reshape(batch, seq, d)
    proj3 = proj.reshape(batch, seq, proj.shape[-1])
    small3 = small.reshape(batch, seq, LANES)
    gw = 3 * D_MODEL // 2
    gblk = C_GATES // gw
    cur = lambda t: (0, jnp.minimum(t, nt - 1), 0)
    prev = lambda t: (0, jnp.maximum(t - 1, 0), 0)
    lvec = lambda t: (layer, 0, 0)
    resident = dict(pipeline_mode=pl.Buffered(1))
    out = pl.pallas_call(
        functools.partial(_mix_body, group=group, merge_rows=merge_rows),
        grid=(nt + 1,),
        in_specs=[
            pl.BlockSpec((batch, ts, C_GATES), cur),
            pl.BlockSpec((batch, ts, LANES), cur),
            pl.BlockSpec((None, 1, LANES), lvec),
            pl.BlockSpec((None, 1, LANES), lvec),
            pl.BlockSpec((None, 1, LANES), lvec),
            pl.BlockSpec((None, 1, WIDTH), lvec),
            pl.BlockSpec((None, N_HEADS, LANES, LANES), lambda t: (layer, 0, 0, 0)),
            pl.BlockSpec((None, 1, WIDTH), lvec),
            pl.BlockSpec((batch, ts, d), prev),
            pl.BlockSpec((batch, ts, gw), lambda t: (0, jnp.maximum(t - 1, 0), gblk)),
            pl.BlockSpec((batch, ts, gw), lambda t: (0, jnp.maximum(t - 1, 0), gblk + 1)),
            pl.BlockSpec((None, WIDTH, d), lvec, **resident),
            pl.BlockSpec((None, WIDTH, d), lvec, **resident),
            pl.BlockSpec((None, WIDTH, d), lvec, **resident),
            pl.BlockSpec((None, d, d), lvec, **resident),
            pl.BlockSpec((None, 1, d), lvec),
        ],
        out_specs=pl.BlockSpec((batch, ts, d), prev),
        out_shape=jax.ShapeDtypeStruct((batch, seq, d), F32),
        scratch_shapes=[
            pltpu.VMEM((batch, N_HEADS, HEAD_DIM, HEAD_DIM), F32),
            pltpu.VMEM((batch, N_HEADS, HEAD_DIM, HEAD_DIM), F32),
            pltpu.VMEM((batch, POOL_HIST, WIDTH), F32),
            pltpu.VMEM((batch, POOL_HIST + ts, WIDTH), F32),
            pltpu.VMEM((2, batch, ts, 3 * WIDTH), BF16),
        ],
        compiler_params=pltpu.CompilerParams(
            dimension_semantics=("arbitrary",), vmem_limit_bytes=VMEM_LIMIT),
        name="mix",
    )(proj3, small3, alog, dtb, dnw, retw, poolw, pscale, x3, proj3, proj3,
      wdn, wpool, wret, wout, pw)
    return out.reshape(batch * seq, d)


def _ffn_body(x_ref, prew_ref, wg_ref, wu_ref, wd_ref, postw_ref, out_ref, h_scr, acc_scr):
    j = pl.program_id(1)

    @pl.when(j == 0)
    def _():
        h_scr[...] = _rms(x_ref[...], prew_ref[...]).astype(BF16)
        acc_scr[...] = jnp.zeros_like(acc_scr)

    h = h_scr[...]
    a = (_silu(_dot(h, wg_ref[...].astype(BF16))) * _dot(h, wu_ref[...].astype(BF16))).astype(BF16)
    acc_scr[...] += _dot(a, wd_ref[...].astype(BF16))

    @pl.when(j == pl.num_programs(1) - 1)
    def _():
        out_ref[...] = x_ref[...] + _rms(acc_scr[...], postw_ref[...])


def _ffn(x, prew, wg, wu, wd, postw, *, layer, tm, tf):
    t, d = x.shape
    f = wg.shape[-1]
    return pl.pallas_call(
        _ffn_body,
        grid=(t // tm, f // tf),
        in_specs=[
            pl.BlockSpec((tm, d), lambda i, j: (i, 0)),
            pl.BlockSpec((None, 1, d), lambda i, j: (layer, 0, 0)),
            pl.BlockSpec((None, d, tf), lambda i, j: (layer, 0, j)),
            pl.BlockSpec((None, d, tf), lambda i, j: (layer, 0, j)),
            pl.BlockSpec((None, tf, d), lambda i, j: (layer, j, 0)),
            pl.BlockSpec((None, 1, d), lambda i, j: (layer, 0, 0)),
        ],
        out_specs=pl.BlockSpec((tm, d), lambda i, j: (i, 0)),
        out_shape=jax.ShapeDtypeStruct((t, d), F32),
        scratch_shapes=[pltpu.VMEM((tm, d), BF16), pltpu.VMEM((tm, d), F32)],
        compiler_params=pltpu.CompilerParams(
            dimension_semantics=("arbitrary", "arbitrary"), vmem_limit_bytes=VMEM_LIMIT),
        name="ffn",
    )(x, prew, wg, wu, wd, postw)


def _rope_tables(seq):
    half = HEAD_DIM // 2
    inv = ROPE_BASE ** (-jnp.arange(half, dtype=F32) / half)
    ang = jnp.arange(seq, dtype=F32)[:, None] * inv[None, :]
    cos = jnp.cos(ang)
    sin = jnp.sin(ang)
    return jnp.concatenate([cos, cos], axis=-1), jnp.concatenate([-sin, sin], axis=-1)


def _lane_vecs(vals):
    return jnp.pad(vals.astype(F32), ((0, 0), (G_LANE, LANES - G_LANE - N_HEADS)))[:, None, :]


def kernel(x, mix_pre_norm, mix_post_norm, w_in, dn_conv, dn_A_log, dn_dt_bias, dn_out_norm, ret_out_norm, pool_w, pool_scale, w_branch_dn, w_branch_ret, w_branch_pool, w_out, ffn_pre_norm, ffn_post_norm, ffn_gate, ffn_up, ffn_down):
    batch, seq, d = x.shape
    depth = w_in.shape[0]
    t = batch * seq
    cos, sin = _rope_tables(seq)
    small_lo = C_DNZ + WIDTH
    small_hi = small_lo + 2 * N_HEADS
    w_main = jnp.concatenate([w_in[:, :, :small_lo], w_in[:, :, small_hi:]], axis=2).astype(BF16)
    w_small = jnp.pad(w_in[:, :, small_lo:small_hi],
                      ((0, 0), (0, 0), (0, LANES - 2 * N_HEADS))).astype(BF16)
    alog = _lane_vecs(dn_A_log)
    dtb = _lane_vecs(dn_dt_bias)
    row3 = lambda a: a[:, None, :]
    wdn, wpool, wret, wout = (a.astype(BF16) for a in (w_branch_dn, w_branch_pool, w_branch_ret, w_out))
    poolw = pool_w.astype(BF16)
    xt = x.reshape(t, d)
    for l in range(depth):
        proj, small = _proj(xt, row3(mix_pre_norm), w_main, w_small, cos, sin, dn_conv,
                            layer=l, seq=seq, tm=1024)
        xt = _mix(xt, proj, small, alog, dtb, row3(dn_out_norm), row3(ret_out_norm), poolw,
                  row3(pool_scale), wdn, wpool, wret, wout, row3(mix_post_norm),
                  layer=l, batch=batch, seq=seq, group=16, merge_rows=4)
        xt = _ffn(xt, row3(ffn_pre_norm), ffn_gate, ffn_up, ffn_down, row3(ffn_post_norm),
                  layer=l, tm=1024, tf=256)
    return xt.reshape(batch, seq, d)
```

```python
import functools
import math

import jax
import jax.numpy as jnp
from jax import lax
from jax.experimental import pallas as pl
from jax.experimental.pallas import tpu as pltpu

F32 = jnp.float32
BF16 = jnp.bfloat16

D_MODEL = 1024
CHUNK = 64
RMS_EPS = 1e-6
GN_EPS = 1e-5
L2_EPS = 1e-6
N_HEADS = 4
HEAD_DIM = 128
DN_QKV = 3 * N_HEADS * HEAD_DIM
WIDTH = N_HEADS * HEAD_DIM
DN_CONV = 4
ROPE_BASE = 10000.0
POOL_WINDOWS = (2, 4, 8, 16)
LANES = 128
CONV_HIST = 8
POOL_HIST = 16

C_QKV = 0
C_DNZ = C_QKV + DN_QKV
C_RQ = C_DNZ + WIDTH
C_RK = C_RQ + WIDTH
C_RV = C_RK + WIDTH
C_RG = C_RV + WIDTH
C_PU = C_RG + WIDTH
C_GATES = C_PU + WIDTH
N_MAIN = C_GATES + 3 * D_MODEL
PROJ_TN = DN_QKV
G_LANE = N_HEADS

VMEM_LIMIT = 56 * 1024 * 1024


def _dot(a, b):
    return jnp.dot(a, b, preferred_element_type=F32)


def _dot_nt(a, b):
    return lax.dot_general(a, b, (((1,), (1,)), ((), ())), preferred_element_type=F32)


def _rms(xf, w):
    return xf * lax.rsqrt(jnp.mean(xf * xf, axis=-1, keepdims=True) + RMS_EPS) * w


def _silu(x):
    return x * jax.nn.sigmoid(x)


def _rotary(x, cos, sin):
    return x * cos + pltpu.roll(x, HEAD_DIM // 2, 1) * sin


def _proj_body(x_ref, nw_ref, w_ref, ws_ref, cos_ref, sin_ref, conv_ref, out_ref, small_ref,
               h_scr, cbuf, hist, *, tiles_per_seq, row_block):
    i = pl.program_id(0)
    j = pl.program_id(1)
    tm = x_ref.shape[0]
    seq_start = lax.rem(i, tiles_per_seq) == 0

    @pl.when((j == 0) & seq_start)
    def _():
        hist[...] = jnp.zeros_like(hist)

    @pl.when(j == 0)
    def _():
        h = _rms(x_ref[...], nw_ref[...]).astype(BF16)
        h_scr[...] = h
        small_ref[...] = _dot(h, ws_ref[...])
        cbuf[0:CONV_HIST, :] = hist[...]
        cbuf[CONV_HIST:CONV_HIST + tm, :] = _dot(h, w_ref[...])
        hist[...] = cbuf[tm:tm + CONV_HIST, :]
        for r0 in range(0, tm, row_block):
            for s in range(DN_QKV // LANES):
                cs = slice(s * LANES, (s + 1) * LANES)
                xb = cbuf[r0:r0 + CONV_HIST + row_block, cs]
                acc = conv_ref[DN_CONV - 1:DN_CONV, cs] * xb
                for k in range(DN_CONV - 1):
                    acc = acc + conv_ref[k:k + 1, cs] * pltpu.roll(xb, DN_CONV - 1 - k, 0)
                y = _silu(acc[CONV_HIST:])
                if s < 2 * N_HEADS:
                    scale = HEAD_DIM ** -0.5 if s < N_HEADS else 1.0
                    y = y * (lax.rsqrt(jnp.sum(y * y, axis=-1, keepdims=True) + L2_EPS) * scale)
                out_ref[r0:r0 + row_block, cs] = y.astype(out_ref.dtype)

    @pl.when(j == 1)
    def _():
        res = _dot(h_scr[...], w_ref[...])
        cos = cos_ref[...]
        sin = sin_ref[...]
        for s in range(PROJ_TN // LANES):
            cs = slice(s * LANES, (s + 1) * LANES)
            v = res[:, cs]
            if s < N_HEADS:
                y = _silu(v)
            elif s < 2 * N_HEADS:
                y = _rotary(v, cos, sin)
            else:
                y = _rotary(v, cos, sin) * (HEAD_DIM ** -0.5)
            out_ref[:, cs] = y.astype(out_ref.dtype)

    @pl.when(j == 2)
    def _():
        res = _dot(h_scr[...], w_ref[...])
        for s in range(PROJ_TN // LANES):
            cs = slice(s * LANES, (s + 1) * LANES)
            v = res[:, cs]
            y = _silu(v) if N_HEADS <= s < 2 * N_HEADS else v
            out_ref[:, cs] = y.astype(out_ref.dtype)

    @pl.when(j >= 3)
    def _():
        out_ref[...] = jax.nn.sigmoid(_dot(h_scr[...], w_ref[...])).astype(out_ref.dtype)


def _proj(x, nw, w_main, w_small, cos, sin, conv, *, layer, seq, tm):
    t, d = x.shape
    tn = PROJ_TN
    n = w_main.shape[-1]
    assert C_DNZ == tn and C_RV == 2 * tn and C_GATES == 3 * tn and n % tn == 0
    tiles_per_seq = seq // tm
    return pl.pallas_call(
        functools.partial(_proj_body, tiles_per_seq=tiles_per_seq, row_block=64),
        grid=(t // tm, n // tn),
        in_specs=[
            pl.BlockSpec((tm, d), lambda i, j: (i, 0)),
            pl.BlockSpec((None, 1, d), lambda i, j: (layer, 0, 0)),
            pl.BlockSpec((None, d, tn), lambda i, j: (layer, 0, j)),
            pl.BlockSpec((None, d, LANES), lambda i, j: (layer, 0, 0)),
            pl.BlockSpec((tm, LANES), lambda i, j: (lax.rem(i, tiles_per_seq), 0)),
            pl.BlockSpec((tm, LANES), lambda i, j: (lax.rem(i, tiles_per_seq), 0)),
            pl.BlockSpec((None, DN_CONV, DN_QKV), lambda i, j: (layer, 0, 0)),
        ],
        out_specs=[
            pl.BlockSpec((tm, tn), lambda i, j: (i, j)),
            pl.BlockSpec((tm, LANES), lambda i, j: (i, 0)),
        ],
        out_shape=[
            jax.ShapeDtypeStruct((t, n), BF16),
            jax.ShapeDtypeStruct((t, LANES), F32),
        ],
        scratch_shapes=[
            pltpu.VMEM((tm, d), BF16),
            pltpu.VMEM((CONV_HIST + tm, DN_QKV), F32),
            pltpu.VMEM((CONV_HIST, DN_QKV), F32),
        ],
        compiler_params=pltpu.CompilerParams(
            dimension_semantics=("arbitrary", "arbitrary"), vmem_limit_bytes=VMEM_LIMIT),
        name="proj",
    )(x, nw, w_main, w_small, cos, sin, conv)


def _merge_rows(x, oc, ga, gb, wdn_ref, wpool_ref, wret_ref, wout_ref, pw_ref):
    d = _dot(oc[:, 0:WIDTH], wdn_ref[...])
    p = _dot(oc[:, WIDTH:2 * WIDTH], wpool_ref[...])
    r = _dot(oc[:, 2 * WIDTH:3 * WIDTH], wret_ref[...])
    ga = ga.astype(F32)
    gb = gb.astype(F32)
    half = D_MODEL // 2
    y_lo = (ga[:, 0:half] * d[:, 0:half] + ga[:, D_MODEL:D_MODEL + half] * p[:, 0:half]
            + gb[:, half:D_MODEL] * r[:, 0:half])
    y_hi = (ga[:, half:D_MODEL] * d[:, half:] + gb[:, 0:half] * p[:, half:]
            + gb[:, D_MODEL:D_MODEL + half] * r[:, half:])
    y = jnp.concatenate([y_lo, y_hi], axis=-1).astype(BF16)
    return x + _rms(_dot(y, wout_ref[...]), pw_ref[...])


def _mix_body(main_ref, small_ref, alog_ref, dtb_ref, dnw_ref,
              retw_ref, poolw_ref, pscale_ref, x_ref, ga_ref, gb_ref, wdn_ref, wpool_ref,
              wret_ref, wout_ref, pw_ref, out_ref,
              s_scr, r_scr, phist, pbuf, ocat_scr, *, group, merge_rows):
    t_idx = pl.program_id(0)
    nb, ts = main_ref.shape[0], main_ref.shape[1]
    assert ts == CHUNK
    slot_w = lax.rem(t_idx, 2)
    slot_r = 1 - slot_w

    @pl.when(t_idx == 0)
    def _():
        s_scr[...] = jnp.zeros_like(s_scr)
        r_scr[...] = jnp.zeros_like(r_scr)
        phist[...] = jnp.zeros_like(phist)
        ocat_scr[...] = jnp.zeros_like(ocat_scr)

    for b0 in range(0, nb, merge_rows):
        bs = slice(b0, b0 + merge_rows)
        rows = merge_rows * ts
        res = _merge_rows(
            x_ref[bs].reshape(rows, D_MODEL),
            ocat_scr[slot_r, bs].reshape(rows, 3 * WIDTH),
            ga_ref[bs].reshape(rows, ga_ref.shape[-1]),
            gb_ref[bs].reshape(rows, gb_ref.shape[-1]),
            wdn_ref, wpool_ref, wret_ref, wout_ref, pw_ref)
        out_ref[bs] = res.reshape(merge_rows, ts, D_MODEL)

    ri = lax.broadcasted_iota(jnp.int32, (CHUNK, CHUNK), 0)
    ci = lax.broadcasted_iota(jnp.int32, (CHUNK, CHUNK), 1)
    causal = ri >= ci
    strict = ri > ci
    eye = jnp.where(ri == ci, 1.0, 0.0).astype(F32)
    tril = jnp.where(causal, 1.0, 0.0).astype(F32)
    absd = jnp.abs(ri - ci).astype(F32)
    rowf = lax.broadcasted_iota(jnp.int32, (CHUNK, LANES), 0).astype(F32)
    chunk_idx = jnp.minimum(t_idx, pl.num_programs(0) - 2)
    pos = rowf + (chunk_idx * ts + 1).astype(F32)
    inv_cnt = [1.0 / jnp.minimum(pos, float(win)) for win in POOL_WINDOWS]

    g_rows = []
    beta = []
    for b in range(nb):
        pbuf[b, 0:POOL_HIST, :] = phist[b]
        pbuf[b, POOL_HIST:POOL_HIST + ts, :] = main_ref[b, :, C_PU:C_PU + WIDTH].astype(F32)
        phist[b] = pbuf[b, ts:ts + POOL_HIST, :]
        for gi in range(len(POOL_WINDOWS)):
            cs = slice(gi * LANES, (gi + 1) * LANES)
            wsum = pbuf[b, :, cs]
            for lvl in range(gi + 1):
                wsum = wsum + pltpu.roll(wsum, 1 << lvl, 0)
            mixed = wsum[POOL_HIST:] * inv_cnt[gi] - pbuf[b, POOL_HIST:POOL_HIST + ts, cs]
            y = _dot(mixed.astype(BF16), poolw_ref[gi]) * pscale_ref[0:1, cs]
            ocat_scr[slot_w, b, :, WIDTH + gi * LANES:WIDTH + (gi + 1) * LANES] = y.astype(BF16)

        sm = small_ref[b]
        beta.append(jax.nn.sigmoid(sm))
        g_rows.append(-(jnp.exp(alog_ref[...]) * jax.nn.softplus(sm + dtb_ref[...])))

    gcol_all = jnp.dot(tril, jnp.concatenate(g_rows, axis=-1),
                       precision=lax.Precision.HIGHEST, preferred_element_type=F32)
    gcol, grow, eg, ktf, dl = [], [], [], [], []
    for b in range(nb):
        gc = gcol_all[:, b * LANES:(b + 1) * LANES]
        gl = gc[CHUNK - 1:CHUNK, :]
        gcol.append(gc)
        grow.append(gc.T[0:2 * N_HEADS, :])
        eg.append(jnp.exp(gc))
        ktf.append(jnp.exp(gl - gc))
        dl.append(jnp.exp(gl))

    ret_consts = []
    for h in range(N_HEADS):
        lg = math.log1p(-(2.0 ** (-5.0 - h)))
        ret_consts.append((jnp.exp(absd * lg), jnp.exp((rowf + 1.0) * lg),
                           jnp.exp((CHUNK - 1.0 - rowf) * lg), math.exp(CHUNK * lg)))

    streams = [(b, h) for b in range(nb) for h in range(N_HEADS)]
    for g0 in range(0, len(streams), group):
        grp = streams[g0:g0 + group]

        st = []
        for b, h in grp:
            col = slice(G_LANE + h, G_LANE + h + 1)
            q16 = main_ref[b, :, h * LANES:(h + 1) * LANES]
            k16 = main_ref[b, :, WIDTH + h * LANES:WIDTH + (h + 1) * LANES]
            q = q16.astype(F32)
            k = k16.astype(F32)
            v = main_ref[b, :, 2 * WIDTH + h * LANES:2 * WIDTH + (h + 1) * LANES].astype(F32)
            diff = gcol[b][:, col] - grow[b][G_LANE + h:G_LANE + h + 1, :]
            decay = jnp.where(causal, jnp.exp(jnp.where(causal, diff, 0.0)), 0.0)
            beta_h = beta[b][:, h:h + 1]
            kb = k * beta_h
            egc = eg[b][:, col]
            kq = _dot_nt(jnp.concatenate([kb.astype(BF16), q16], axis=0), k16)
            p = -jnp.where(strict, kq[0:CHUNK] * decay, 0.0)
            st.append(dict(
                p=p, t=eye + p,
                attn16=(kq[CHUNK:2 * CHUNK] * decay).astype(BF16),
                rhs16=jnp.concatenate([v * beta_h, kb * egc], axis=-1).astype(BF16),
                qd16=(q * egc).astype(BF16),
                ktT16=(k * ktf[b][:, col]).T.astype(BF16)))

        for _ in range(5):
            for s in st:
                p16 = s["p"].astype(BF16)
                s["p"] = _dot(p16, p16)
            for s in st:
                s["t"] = s["t"] + _dot(s["t"].astype(BF16), s["p"].astype(BF16))
        for s in st:
            s["sol"] = _dot(s["t"].astype(BF16), s["rhs16"])

        for s, (b, h) in zip(st, grp):
            state = s_scr[b, h]
            s["state"] = state
            lhs = jnp.concatenate([s["sol"][:, HEAD_DIM:].astype(BF16), s["qd16"]], axis=0)
            s["ws"] = _dot(lhs, state.astype(BF16))
        for s, (b, h) in zip(st, grp):
            col = slice(G_LANE + h, G_LANE + h + 1)
            v_new = s["sol"][:, :HEAD_DIM] - s["ws"][0:CHUNK]
            lhs = jnp.concatenate([s["attn16"], s["ktT16"]], axis=0)
            res = _dot(lhs, v_new.astype(BF16))
            s_scr[b, h] = s["state"] * dl[b][:, col] + res[CHUNK:]
            o = s["ws"][CHUNK:] + res[0:CHUNK]
            o = o * lax.rsqrt(jnp.mean(o * o, axis=-1, keepdims=True) + RMS_EPS) * dnw_ref[...]
            z = main_ref[b, :, C_DNZ + h * LANES:C_DNZ + (h + 1) * LANES].astype(F32)
            ocat_scr[slot_w, b, :, h * LANES:(h + 1) * LANES] = (o * z).astype(BF16)

        rt = []
        for b, h in grp:
            dmask, xi, zeta, _ = ret_consts[h]
            rq16 = main_ref[b, :, C_RQ + h * LANES:C_RQ + (h + 1) * LANES]
            rk16 = main_ref[b, :, C_RK + h * LANES:C_RK + (h + 1) * LANES]
            rstate = r_scr[b, h]
            rt.append(dict(
                scores=_dot_nt(rq16, rk16) * dmask,
                cross=_dot((rq16.astype(F32) * xi).astype(BF16), rstate.astype(BF16)),
                rkzT16=(rk16.astype(F32) * zeta).T.astype(BF16), rstate=rstate))
        for s, (b, h) in zip(rt, grp):
            hs = slice(h * LANES, (h + 1) * LANES)
            rv16 = main_ref[b, :, C_RV + h * LANES:C_RV + (h + 1) * LANES]
            lhs = jnp.concatenate([s["scores"].astype(BF16), s["rkzT16"]], axis=0)
            res = _dot(lhs, rv16)
            r_scr[b, h] = s["rstate"] * ret_consts[h][3] + res[CHUNK:]
            o_r = res[0:CHUNK] + s["cross"]
            mu = jnp.mean(o_r, axis=-1, keepdims=True)
            dev = o_r - mu
            var = jnp.mean(dev * dev, axis=-1, keepdims=True)
            o_r = dev * lax.rsqrt(var + GN_EPS) * retw_ref[0:1, hs]
            gr = main_ref[b, :, C_RG + h * LANES:C_RG + (h + 1) * LANES].astype(F32)
            ocat_scr[slot_w, b, :, 2 * WIDTH + h * LANES:2 * WIDTH + (h + 1) * LANES] = (
                o_r * gr).astype(BF16)


def _mix(x, proj, small, alog, dtb, dnw, retw, poolw, pscale, wdn, wpool, wret, wout, pw,
         *, layer, batch, seq, group, merge_rows):
    ts = CHUNK
    nt = seq // ts
    d = x.shape[-1]
    x3 = x.reshape(batch, seq, d)
    proj3 = proj.reshape(batch, seq, proj.shape[-1])
    small3 = small.reshape(batch, seq, LANES)
    gw = 3 * D_MODEL // 2
    gblk = C_GATES // gw
    cur = lambda t: (0, jnp.minimum(t, nt - 1), 0)
    prev = lambda t: (0, jnp.maximum(t - 1, 0), 0)
    lvec = lambda t: (layer, 0, 0)
    resident = dict(pipeline_mode=pl.Buffered(1))
    out = pl.pallas_call(
        functools.partial(_mix_body, group=group, merge_rows=merge_rows),
        grid=(nt + 1,),
        in_specs=[
            pl.BlockSpec((batch, ts, C_GATES), cur),
            pl.BlockSpec((batch, ts, LANES), cur),
            pl.BlockSpec((None, 1, LANES), lvec),
            pl.BlockSpec((None, 1, LANES), lvec),
            pl.BlockSpec((None, 1, LANES), lvec),
            pl.BlockSpec((None, 1, WIDTH), lvec),
            pl.BlockSpec((None, N_HEADS, LANES, LANES), lambda t: (layer, 0, 0, 0)),
            pl.BlockSpec((None, 1, WIDTH), lvec),
            pl.BlockSpec((batch, ts, d), prev),
            pl.BlockSpec((batch, ts, gw), lambda t: (0, jnp.maximum(t - 1, 0), gblk)),
            pl.BlockSpec((batch, ts, gw), lambda t: (0, jnp.maximum(t - 1, 0), gblk + 1)),
            pl.BlockSpec((None, WIDTH, d), lvec, **resident),
            pl.BlockSpec((None, WIDTH, d), lvec, **resident),
            pl.BlockSpec((None, WIDTH, d), lvec, **resident),
            pl.BlockSpec((None, d, d), lvec, **resident),
            pl.BlockSpec((None, 1, d), lvec),
        ],
        out_specs=pl.BlockSpec((batch, ts, d), prev),
        out_shape=jax.ShapeDtypeStruct((batch, seq, d), F32),
        scratch_shapes=[
            pltpu.VMEM((batch, N_HEADS, HEAD_DIM, HEAD_DIM), F32),
            pltpu.VMEM((batch, N_HEADS, HEAD_DIM, HEAD_DIM), F32),
            pltpu.VMEM((batch, POOL_HIST, WIDTH), F32),
            pltpu.VMEM((batch, POOL_HIST + ts, WIDTH), F32),
            pltpu.VMEM((2, batch, ts, 3 * WIDTH), BF16),
        ],
        compiler_params=pltpu.CompilerParams(
            dimension_semantics=("arbitrary",), vmem_limit_bytes=VMEM_LIMIT),
        name="mix",
    )(proj3, small3, alog, dtb, dnw, retw, poolw, pscale, x3, proj3, proj3,
      wdn, wpool, wret, wout, pw)
    return out.reshape(batch * seq, d)


def _ffn_body(x_ref, prew_ref, wg_ref, wu_ref, wd_ref, postw_ref, out_ref, h_scr, acc_scr):
    j = pl.program_id(1)

    @pl.when(j == 0)
    def _():
        h_scr[...] = _rms(x_ref[...], prew_ref[...]).astype(BF16)
        acc_scr[...] = jnp.zeros_like(acc_scr)

    h = h_scr[...]
    a = (_silu(_dot(h, wg_ref[...].astype(BF16))) * _dot(h, wu_ref[...].astype(BF16))).astype(BF16)
    acc_scr[...] += _dot(a, wd_ref[...].astype(BF16))

    @pl.when(j == pl.num_programs(1) - 1)
    def _():
        out_ref[...] = x_ref[...] + _rms(acc_scr[...], postw_ref[...])


def _ffn(x, prew, wg, wu, wd, postw, *, layer, tm, tf):
    t, d = x.shape
    f = wg.shape[-1]
    return pl.pallas_call(
        _ffn_body,
        grid=(t // tm, f // tf),
        in_specs=[
            pl.BlockSpec((tm, d), lambda i, j: (i, 0)),
            pl.BlockSpec((None, 1, d), lambda i, j: (layer, 0, 0)),
            pl.BlockSpec((None, d, tf), lambda i, j: (layer, 0, j)),
            pl.BlockSpec((None, d, tf), lambda i, j: (layer, 0, j)),
            pl.BlockSpec((None, tf, d), lambda i, j: (layer, j, 0)),
            pl.BlockSpec((None, 1, d), lambda i, j: (layer, 0, 0)),
        ],
        out_specs=pl.BlockSpec((tm, d), lambda i, j: (i, 0)),
        out_shape=jax.ShapeDtypeStruct((t, d), F32),
        scratch_shapes=[pltpu.VMEM((tm, d), BF16), pltpu.VMEM((tm, d), F32)],
        compiler_params=pltpu.CompilerParams(
            dimension_semantics=("arbitrary", "arbitrary"), vmem_limit_bytes=VMEM_LIMIT),
        name="ffn",
    )(x, prew, wg, wu, wd, postw)


def _rope_tables(seq):
    half = HEAD_DIM // 2
    inv = ROPE_BASE ** (-jnp.arange(half, dtype=F32) / half)
    ang = jnp.arange(seq, dtype=F32)[:, None] * inv[None, :]
    cos = jnp.cos(ang)
    sin = jnp.sin(ang)
    return jnp.concatenate([cos, cos], axis=-1), jnp.concatenate([-sin, sin], axis=-1)


def _lane_vecs(vals):
    return jnp.pad(vals.astype(F32), ((0, 0), (G_LANE, LANES - G_LANE - N_HEADS)))[:, None, :]


def kernel(x, mix_pre_norm, mix_post_norm, w_in, dn_conv, dn_A_log, dn_dt_bias, dn_out_norm, ret_out_norm, pool_w, pool_scale, w_branch_dn, w_branch_ret, w_branch_pool, w_out, ffn_pre_norm, ffn_post_norm, ffn_gate, ffn_up, ffn_down):
    batch, seq, d = x.shape
    depth = w_in.shape[0]
    t = batch * seq
    cos, sin = _rope_tables(seq)
    small_lo = C_DNZ + WIDTH
    small_hi = small_lo + 2 * N_HEADS
    w_main = jnp.concatenate([w_in[:, :, :small_lo], w_in[:, :, small_hi:]], axis=2).astype(BF16)
    w_small = jnp.pad(w_in[:, :, small_lo:small_hi],
                      ((0, 0), (0, 0), (0, LANES - 2 * N_HEADS))).astype(BF16)
    alog = _lane_vecs(dn_A_log)
    dtb = _lane_vecs(dn_dt_bias)
    row3 = lambda a: a[:, None, :]
    wdn, wpool, wret, wout = (a.astype(BF16) for a in (w_branch_dn, w_branch_pool, w_branch_ret, w_out))
    poolw = pool_w.astype(BF16)
    xt = x.reshape(t, d)
    for l in range(depth):
        proj, small = _proj(xt, row3(mix_pre_norm), w_main, w_small, cos, sin, dn_conv,
                            layer=l, seq=seq, tm=1024)
        xt = _mix(xt, proj, small, alog, dtb, row3(dn_out_norm), row3(ret_out_norm), poolw,
                  row3(pool_scale), wdn, wpool, wret, wout, row3(mix_post_norm),
                  layer=l, batch=batch, seq=seq, group=16, merge_rows=8)
        xt = _ffn(xt, row3(ffn_pre_norm), ffn_gate, ffn_up, ffn_down, row3(ffn_post_norm),
                  layer=l, tm=1024, tf=256)
    return xt.reshape(batch, seq, d)
```

```python
import functools
import math

import jax
import jax.numpy as jnp
from jax import lax
from jax.experimental import pallas as pl
from jax.experimental.pallas import tpu as pltpu

F32 = jnp.float32
BF16 = jnp.bfloat16

D_MODEL = 1024
CHUNK = 64
RMS_EPS = 1e-6
GN_EPS = 1e-5
L2_EPS = 1e-6
N_HEADS = 4
HEAD_DIM = 128
DN_QKV = 3 * N_HEADS * HEAD_DIM
WIDTH = N_HEADS * HEAD_DIM
DN_CONV = 4
ROPE_BASE = 10000.0
POOL_WINDOWS = (2, 4, 8, 16)
LANES = 128
CONV_HIST = 8
POOL_HIST = 16

C_QKV = 0
C_DNZ = C_QKV + DN_QKV
C_RQ = C_DNZ + WIDTH
C_RK = C_RQ + WIDTH
C_RV = C_RK + WIDTH
C_RG = C_RV + WIDTH
C_PU = C_RG + WIDTH
C_GATES = C_PU + WIDTH
N_MAIN = C_GATES + 3 * D_MODEL
PROJ_TN = DN_QKV
G_LANE = N_HEADS

VMEM_LIMIT = 56 * 1024 * 1024


def _dot(a, b):
    return jnp.dot(a, b, preferred_element_type=F32)


def _dot_nt(a, b):
    return lax.dot_general(a, b, (((1,), (1,)), ((), ())), preferred_element_type=F32)


def _rms(xf, w):
    return xf * lax.rsqrt(jnp.mean(xf * xf, axis=-1, keepdims=True) + RMS_EPS) * w


def _silu(x):
    return x * jax.nn.sigmoid(x)


def _rotary(x, cos, sin):
    return x * cos + pltpu.roll(x, HEAD_DIM // 2, 1) * sin


def _proj_body(x_ref, nw_ref, w_ref, ws_ref, cos_ref, sin_ref, conv_ref, out_ref, small_ref,
               h_scr, cbuf, hist, *, tiles_per_seq, row_block):
    i = pl.program_id(0)
    j = pl.program_id(1)
    tm = x_ref.shape[0]
    seq_start = lax.rem(i, tiles_per_seq) == 0

    @pl.when((j == 0) & seq_start)
    def _():
        hist[...] = jnp.zeros_like(hist)

    @pl.when(j == 0)
    def _():
        h = _rms(x_ref[...], nw_ref[...]).astype(BF16)
        h_scr[...] = h
        small_ref[...] = _dot(h, ws_ref[...])
        cbuf[0:CONV_HIST, :] = hist[...]
        cbuf[CONV_HIST:CONV_HIST + tm, :] = _dot(h, w_ref[...])
        hist[...] = cbuf[tm:tm + CONV_HIST, :]
        for r0 in range(0, tm, row_block):
            for s in range(DN_QKV // LANES):
                cs = slice(s * LANES, (s + 1) * LANES)
                xb = cbuf[r0:r0 + CONV_HIST + row_block, cs]
                acc = conv_ref[DN_CONV - 1:DN_CONV, cs] * xb
                for k in range(DN_CONV - 1):
                    acc = acc + conv_ref[k:k + 1, cs] * pltpu.roll(xb, DN_CONV - 1 - k, 0)
                y = _silu(acc[CONV_HIST:])
                if s < 2 * N_HEADS:
                    scale = HEAD_DIM ** -0.5 if s < N_HEADS else 1.0
                    y = y * (lax.rsqrt(jnp.sum(y * y, axis=-1, keepdims=True) + L2_EPS) * scale)
                out_ref[r0:r0 + row_block, cs] = y.astype(out_ref.dtype)

    @pl.when(j == 1)
    def _():
        res = _dot(h_scr[...], w_ref[...])
        cos = cos_ref[...]
        sin = sin_ref[...]
        for s in range(PROJ_TN // LANES):
            cs = slice(s * LANES, (s + 1) * LANES)
            v = res[:, cs]
            if s < N_HEADS:
                y = _silu(v)
            elif s < 2 * N_HEADS:
                y = _rotary(v, cos, sin)
            else:
                y = _rotary(v, cos, sin) * (HEAD_DIM ** -0.5)
            out_ref[:, cs] = y.astype(out_ref.dtype)

    @pl.when(j == 2)
    def _():
        res = _dot(h_scr[...], w_ref[...])
        for s in range(PROJ_TN // LANES):
            cs = slice(s * LANES, (s + 1) * LANES)
            v = res[:, cs]
            y = _silu(v) if N_HEADS <= s < 2 * N_HEADS else v
            out_ref[:, cs] = y.astype(out_ref.dtype)

    @pl.when(j >= 3)
    def _():
        out_ref[...] = jax.nn.sigmoid(_dot(h_scr[...], w_ref[...])).astype(out_ref.dtype)


def _proj(x, nw, w_main, w_small, cos, sin, conv, *, layer, seq, tm):
    t, d = x.shape
    tn = PROJ_TN
    n = w_main.shape[-1]
    assert C_DNZ == tn and C_RV == 2 * tn and C_GATES == 3 * tn and n % tn == 0
    tiles_per_seq = seq // tm
    return pl.pallas_call(
        functools.partial(_proj_body, tiles_per_seq=tiles_per_seq, row_block=64),
        grid=(t // tm, n // tn),
        in_specs=[
            pl.BlockSpec((tm, d), lambda i, j: (i, 0)),
            pl.BlockSpec((None, 1, d), lambda i, j: (layer, 0, 0)),
            pl.BlockSpec((None, d, tn), lambda i, j: (layer, 0, j)),
            pl.BlockSpec((None, d, LANES), lambda i, j: (layer, 0, 0)),
            pl.BlockSpec((tm, LANES), lambda i, j: (lax.rem(i, tiles_per_seq), 0)),
            pl.BlockSpec((tm, LANES), lambda i, j: (lax.rem(i, tiles_per_seq), 0)),
            pl.BlockSpec((None, DN_CONV, DN_QKV), lambda i, j: (layer, 0, 0)),
        ],
        out_specs=[
            pl.BlockSpec((tm, tn), lambda i, j: (i, j)),
            pl.BlockSpec((tm, LANES), lambda i, j: (i, 0)),
        ],
        out_shape=[
            jax.ShapeDtypeStruct((t, n), BF16),
            jax.ShapeDtypeStruct((t, LANES), F32),
        ],
        scratch_shapes=[
            pltpu.VMEM((tm, d), BF16),
            pltpu.VMEM((CONV_HIST + tm, DN_QKV), F32),
            pltpu.VMEM((CONV_HIST, DN_QKV), F32),
        ],
        compiler_params=pltpu.CompilerParams(
            dimension_semantics=("arbitrary", "arbitrary"), vmem_limit_bytes=VMEM_LIMIT),
        name="proj",
    )(x, nw, w_main, w_small, cos, sin, conv)


def _merge_rows(x, oc, ga, gb, wdn_ref, wpool_ref, wret_ref, wout_ref, pw_ref):
    d = _dot(oc[:, 0:WIDTH], wdn_ref[...])
    p = _dot(oc[:, WIDTH:2 * WIDTH], wpool_ref[...])
    r = _dot(oc[:, 2 * WIDTH:3 * WIDTH], wret_ref[...])
    ga = ga.astype(F32)
    gb = gb.astype(F32)
    half = D_MODEL // 2
    y_lo = (ga[:, 0:half] * d[:, 0:half] + ga[:, D_MODEL:D_MODEL + half] * p[:, 0:half]
            + gb[:, half:D_MODEL] * r[:, 0:half])
    y_hi = (ga[:, half:D_MODEL] * d[:, half:] + gb[:, 0:half] * p[:, half:]
            + gb[:, D_MODEL:D_MODEL + half] * r[:, half:])
    y = jnp.concatenate([y_lo, y_hi], axis=-1).astype(BF16)
    return x + _rms(_dot(y, wout_ref[...]), pw_ref[...])


def _mix_body(main_ref, small_ref, alog_ref, dtb_ref, dnw_ref,
              retw_ref, poolw_ref, pscale_ref, x_ref, ga_ref, gb_ref, wdn_ref, wpool_ref,
              wret_ref, wout_ref, pw_ref, out_ref,
              s_scr, r_scr, phist, pbuf, ocat_scr, *, group, merge_rows):
    t_idx = pl.program_id(0)
    nb, ts = main_ref.shape[0], main_ref.shape[1]
    assert ts == CHUNK
    slot_w = lax.rem(t_idx, 2)
    slot_r = 1 - slot_w

    @pl.when(t_idx == 0)
    def _():
        s_scr[...] = jnp.zeros_like(s_scr)
        r_scr[...] = jnp.zeros_like(r_scr)
        phist[...] = jnp.zeros_like(phist)
        ocat_scr[...] = jnp.zeros_like(ocat_scr)

    for b0 in range(0, nb, merge_rows):
        bs = slice(b0, b0 + merge_rows)
        rows = merge_rows * ts
        res = _merge_rows(
            x_ref[bs].reshape(rows, D_MODEL),
            ocat_scr[slot_r, bs].reshape(rows, 3 * WIDTH),
            ga_ref[bs].reshape(rows, ga_ref.shape[-1]),
            gb_ref[bs].reshape(rows, gb_ref.shape[-1]),
            wdn_ref, wpool_ref, wret_ref, wout_ref, pw_ref)
        out_ref[bs] = res.reshape(merge_rows, ts, D_MODEL)

    ri = lax.broadcasted_iota(jnp.int32, (CHUNK, CHUNK), 0)
    ci = lax.broadcasted_iota(jnp.int32, (CHUNK, CHUNK), 1)
    causal = ri >= ci
    strict = ri > ci
    eye = jnp.where(ri == ci, 1.0, 0.0).astype(F32)
    tril = jnp.where(causal, 1.0, 0.0).astype(F32)
    absd = jnp.abs(ri - ci).astype(F32)
    rowf = lax.broadcasted_iota(jnp.int32, (CHUNK, LANES), 0).astype(F32)
    chunk_idx = jnp.minimum(t_idx, pl.num_programs(0) - 2)
    pos = rowf + (chunk_idx * ts + 1).astype(F32)
    inv_cnt = [1.0 / jnp.minimum(pos, float(win)) for win in POOL_WINDOWS]

    g_rows = []
    beta = []
    for b in range(nb):
        pbuf[b, 0:POOL_HIST, :] = phist[b]
        pbuf[b, POOL_HIST:POOL_HIST + ts, :] = main_ref[b, :, C_PU:C_PU + WIDTH].astype(F32)
        phist[b] = pbuf[b, ts:ts + POOL_HIST, :]
        for gi in range(len(POOL_WINDOWS)):
            cs = slice(gi * LANES, (gi + 1) * LANES)
            wsum = pbuf[b, :, cs]
            for lvl in range(gi + 1):
                wsum = wsum + pltpu.roll(wsum, 1 << lvl, 0)
            mixed = wsum[POOL_HIST:] * inv_cnt[gi] - pbuf[b, POOL_HIST:POOL_HIST + ts, cs]
            y = _dot(mixed.astype(BF16), poolw_ref[gi]) * pscale_ref[0:1, cs]
            ocat_scr[slot_w, b, :, WIDTH + gi * LANES:WIDTH + (gi + 1) * LANES] = y.astype(BF16)

        sm = small_ref[b]
        beta.append(jax.nn.sigmoid(sm))
        g_rows.append(-(jnp.exp(alog_ref[...]) * jax.nn.softplus(sm + dtb_ref[...])))

    gcol_all = jnp.dot(tril, jnp.concatenate(g_rows, axis=-1),
                       precision=lax.Precision.HIGHEST, preferred_element_type=F32)
    gcol, grow, eg, ktf, dl = [], [], [], [], []
    for b in range(nb):
        gc = gcol_all[:, b * LANES:(b + 1) * LANES]
        gl = gc[CHUNK - 1:CHUNK, :]
        gcol.append(gc)
        grow.append(gc.T[0:2 * N_HEADS, :])
        eg.append(jnp.exp(gc))
        ktf.append(jnp.exp(gl - gc))
        dl.append(jnp.exp(gl))

    ret_consts = []
    for h in range(N_HEADS):
        lg = math.log1p(-(2.0 ** (-5.0 - h)))
        ret_consts.append((jnp.exp(absd * lg), jnp.exp((rowf + 1.0) * lg),
                           jnp.exp((CHUNK - 1.0 - rowf) * lg), math.exp(CHUNK * lg)))

    streams = [(b, h) for b in range(nb) for h in range(N_HEADS)]
    for g0 in range(0, len(streams), group):
        grp = streams[g0:g0 + group]

        st = []
        for b, h in grp:
            col = slice(G_LANE + h, G_LANE + h + 1)
            q16 = main_ref[b, :, h * LANES:(h + 1) * LANES]
            k16 = main_ref[b, :, WIDTH + h * LANES:WIDTH + (h + 1) * LANES]
            q = q16.astype(F32)
            k = k16.astype(F32)
            v = main_ref[b, :, 2 * WIDTH + h * LANES:2 * WIDTH + (h + 1) * LANES].astype(F32)
            diff = gcol[b][:, col] - grow[b][G_LANE + h:G_LANE + h + 1, :]
            decay = jnp.where(causal, jnp.exp(jnp.where(causal, diff, 0.0)), 0.0)
            beta_h = beta[b][:, h:h + 1]
            kb = k * beta_h
            egc = eg[b][:, col]
            kq = _dot_nt(jnp.concatenate([kb.astype(BF16), q16], axis=0), k16)
            p = -jnp.where(strict, kq[0:CHUNK] * decay, 0.0)
            st.append(dict(
                p=p, t=eye + p,
                attn16=(kq[CHUNK:2 * CHUNK] * decay).astype(BF16),
                rhs16=jnp.concatenate([v * beta_h, kb * egc], axis=-1).astype(BF16),
                qd16=(q * egc).astype(BF16),
                ktT16=(k * ktf[b][:, col]).T.astype(BF16)))

        for _ in range(5):
            for s in st:
                p16 = s["p"].astype(BF16)
                s["p"] = _dot(p16, p16)
            for s in st:
                s["t"] = s["t"] + _dot(s["t"].astype(BF16), s["p"].astype(BF16))
        for s in st:
            s["sol"] = _dot(s["t"].astype(BF16), s["rhs16"])

        for s, (b, h) in zip(st, grp):
            state = s_scr[b, h]
            s["state"] = state
            lhs = jnp.concatenate([s["sol"][:, HEAD_DIM:].astype(BF16), s["qd16"]], axis=0)
            s["ws"] = _dot(lhs, state.astype(BF16))
        for s, (b, h) in zip(st, grp):
            col = slice(G_LANE + h, G_LANE + h + 1)
            v_new = s["sol"][:, :HEAD_DIM] - s["ws"][0:CHUNK]
            lhs = jnp.concatenate([s["attn16"], s["ktT16"]], axis=0)
            res = _dot(lhs, v_new.astype(BF16))
            s_scr[b, h] = s["state"] * dl[b][:, col] + res[CHUNK:]
            o = s["ws"][CHUNK:] + res[0:CHUNK]
            o = o * lax.rsqrt(jnp.mean(o * o, axis=-1, keepdims=True) + RMS_EPS) * dnw_ref[...]
            z = main_ref[b, :, C_DNZ + h * LANES:C_DNZ + (h + 1) * LANES].astype(F32)
            ocat_scr[slot_w, b, :, h * LANES:(h + 1) * LANES] = (o * z).astype(BF16)

        rt = []
        for b, h in grp:
            dmask, xi, zeta, _ = ret_consts[h]
            rq16 = main_ref[b, :, C_RQ + h * LANES:C_RQ + (h + 1) * LANES]
            rk16 = main_ref[b, :, C_RK + h * LANES:C_RK + (h + 1) * LANES]
            rstate = r_scr[b, h]
            rt.append(dict(
                scores=_dot_nt(rq16, rk16) * dmask,
                cross=_dot((rq16.astype(F32) * xi).astype(BF16), rstate.astype(BF16)),
                rkzT16=(rk16.astype(F32) * zeta).T.astype(BF16), rstate=rstate))
        for s, (b, h) in zip(rt, grp):
            hs = slice(h * LANES, (h + 1) * LANES)
            rv16 = main_ref[b, :, C_RV + h * LANES:C_RV + (h + 1) * LANES]
            lhs = jnp.concatenate([s["scores"].astype(BF16), s["rkzT16"]], axis=0)
            res = _dot(lhs, rv16)
            r_scr[b, h] = s["rstate"] * ret_consts[h][3] + res[CHUNK:]
            o_r = res[0:CHUNK] + s["cross"]
            mu = jnp.mean(o_r, axis=-1, keepdims=True)
            dev = o_r - mu
            var = jnp.mean(dev * dev, axis=-1, keepdims=True)
            o_r = dev * lax.rsqrt(var + GN_EPS) * retw_ref[0:1, hs]
            gr = main_ref[b, :, C_RG + h * LANES:C_RG + (h + 1) * LANES].astype(F32)
            ocat_scr[slot_w, b, :, 2 * WIDTH + h * LANES:2 * WIDTH + (h + 1) * LANES] = (
                o_r * gr).astype(BF16)


def _mix(x, proj, small, alog, dtb, dnw, retw, poolw, pscale, wdn, wpool, wret, wout, pw,
         *, layer, batch, seq, group, merge_rows):
    ts = CHUNK
    nt = seq // ts
    d = x.shape[-1]
    x3 = x.reshape(batch, seq, d)
    proj3 = proj.reshape(batch, seq, proj.shape[-1])
    small3 = small.reshape(batch, seq, LANES)
    gw = 3 * D_MODEL // 2
    gblk = C_GATES // gw
    cur = lambda t: (0, jnp.minimum(t, nt - 1), 0)
    prev = lambda t: (0, jnp.maximum(t - 1, 0), 0)
    lvec = lambda t: (layer, 0, 0)
    resident = dict(pipeline_mode=pl.Buffered(1))
    out = pl.pallas_call(
        functools.partial(_mix_body, group=group, merge_rows=merge_rows),
        grid=(nt + 1,),
        in_specs=[
            pl.BlockSpec((batch, ts, C_GATES), cur),
            pl.BlockSpec((batch, ts, LANES), cur),
            pl.BlockSpec((None, 1, LANES), lvec),
            pl.BlockSpec((None, 1, LANES), lvec),
            pl.BlockSpec((None, 1, LANES), lvec),
            pl.BlockSpec((None, 1, WIDTH), lvec),
            pl.BlockSpec((None, N_HEADS, LANES, LANES), lambda t: (layer, 0, 0, 0)),
            pl.BlockSpec((None, 1, WIDTH), lvec),
            pl.BlockSpec((batch, ts, d), prev),
            pl.BlockSpec((batch, ts, gw), lambda t: (0, jnp.maximum(t - 1, 0), gblk)),
            pl.BlockSpec((batch, ts, gw), lambda t: (0, jnp.maximum(t - 1, 0), gblk + 1)),
            pl.BlockSpec((None, WIDTH, d), lvec, **resident),
            pl.BlockSpec((None, WIDTH, d), lvec, **resident),
            pl.BlockSpec((None, WIDTH, d), lvec, **resident),
            pl.BlockSpec((None, d, d), lvec, **resident),
            pl.BlockSpec((None, 1, d), lvec),
        ],
        out_specs=pl.BlockSpec((batch, ts, d), prev),
        out_shape=jax.ShapeDtypeStruct((batch, seq, d), F32),
        scratch_shapes=[
            pltpu.VMEM((batch, N_HEADS, HEAD_DIM, HEAD_DIM), F32),
            pltpu.VMEM((batch, N_HEADS, HEAD_DIM, HEAD_DIM), F32),
            pltpu.VMEM((batch, POOL_HIST, WIDTH), F32),
            pltpu.VMEM((batch, POOL_HIST + ts, WIDTH), F32),
            pltpu.VMEM((2, batch, ts, 3 * WIDTH), BF16),
        ],
        compiler_params=pltpu.CompilerParams(
            dimension_semantics=("arbitrary",), vmem_limit_bytes=VMEM_LIMIT),
        name="mix",
    )(proj3, small3, alog, dtb, dnw, retw, poolw, pscale, x3, proj3, proj3,
      wdn, wpool, wret, wout, pw)
    return out.reshape(batch * seq, d)


def _ffn_body(x_ref, xn_ref, prew_ref, wg_ref, wu_ref, wd_ref, postw_ref, out_ref,
              h_scr, hn_scr, acc_scr, *, norm_rows):
    i = pl.program_id(0)
    j = pl.program_id(1)
    tm = x_ref.shape[0]

    @pl.when((j == 0) & (i == 0))
    def _():
        h_scr[...] = _rms(x_ref[...], prew_ref[...]).astype(BF16)

    @pl.when((j == 0) & (i > 0))
    def _():
        h_scr[...] = hn_scr[...]

    @pl.when(j == 0)
    def _():
        acc_scr[...] = jnp.zeros_like(acc_scr)

    h = h_scr[...]
    a = (_silu(_dot(h, wg_ref[...].astype(BF16))) * _dot(h, wu_ref[...].astype(BF16))).astype(BF16)
    acc_scr[...] += _dot(a, wd_ref[...].astype(BF16))

    r0 = pl.multiple_of(jnp.minimum(j, tm // norm_rows - 1) * norm_rows, norm_rows)
    hn_scr[pl.ds(r0, norm_rows), :] = _rms(xn_ref[pl.ds(r0, norm_rows), :], prew_ref[...]).astype(BF16)

    @pl.when(j == pl.num_programs(1) - 1)
    def _():
        out_ref[...] = x_ref[...] + _rms(acc_scr[...], postw_ref[...])


def _ffn(x, prew, wg, wu, wd, postw, *, layer, tm, tf, norm_rows):
    t, d = x.shape
    f = wg.shape[-1]
    assert f // tf >= tm // norm_rows
    last = t // tm - 1
    return pl.pallas_call(
        functools.partial(_ffn_body, norm_rows=norm_rows),
        grid=(t // tm, f // tf),
        in_specs=[
            pl.BlockSpec((tm, d), lambda i, j: (i, 0)),
            pl.BlockSpec((tm, d), lambda i, j: (jnp.minimum(i + 1, last), 0)),
            pl.BlockSpec((None, 1, d), lambda i, j: (layer, 0, 0)),
            pl.BlockSpec((None, d, tf), lambda i, j: (layer, 0, j)),
            pl.BlockSpec((None, d, tf), lambda i, j: (layer, 0, j)),
            pl.BlockSpec((None, tf, d), lambda i, j: (layer, j, 0)),
            pl.BlockSpec((None, 1, d), lambda i, j: (layer, 0, 0)),
        ],
        out_specs=pl.BlockSpec((tm, d), lambda i, j: (i, 0)),
        out_shape=jax.ShapeDtypeStruct((t, d), F32),
        scratch_shapes=[pltpu.VMEM((tm, d), BF16), pltpu.VMEM((tm, d), BF16),
                        pltpu.VMEM((tm, d), F32)],
        compiler_params=pltpu.CompilerParams(
            dimension_semantics=("arbitrary", "arbitrary"), vmem_limit_bytes=VMEM_LIMIT),
        name="ffn",
    )(x, x, prew, wg, wu, wd, postw)


def _rope_tables(seq):
    half = HEAD_DIM // 2
    inv = ROPE_BASE ** (-jnp.arange(half, dtype=F32) / half)
    ang = jnp.arange(seq, dtype=F32)[:, None] * inv[None, :]
    cos = jnp.cos(ang)
    sin = jnp.sin(ang)
    return jnp.concatenate([cos, cos], axis=-1), jnp.concatenate([-sin, sin], axis=-1)


def _lane_vecs(vals):
    return jnp.pad(vals.astype(F32), ((0, 0), (G_LANE, LANES - G_LANE - N_HEADS)))[:, None, :]


def kernel(x, mix_pre_norm, mix_post_norm, w_in, dn_conv, dn_A_log, dn_dt_bias, dn_out_norm, ret_out_norm, pool_w, pool_scale, w_branch_dn, w_branch_ret, w_branch_pool, w_out, ffn_pre_norm, ffn_post_norm, ffn_gate, ffn_up, ffn_down):
    batch, seq, d = x.shape
    depth = w_in.shape[0]
    t = batch * seq
    cos, sin = _rope_tables(seq)
    small_lo = C_DNZ + WIDTH
    small_hi = small_lo + 2 * N_HEADS
    w_main = jnp.concatenate([w_in[:, :, :small_lo], w_in[:, :, small_hi:]], axis=2).astype(BF16)
    w_small = jnp.pad(w_in[:, :, small_lo:small_hi],
                      ((0, 0), (0, 0), (0, LANES - 2 * N_HEADS))).astype(BF16)
    alog = _lane_vecs(dn_A_log)
    dtb = _lane_vecs(dn_dt_bias)
    row3 = lambda a: a[:, None, :]
    wdn, wpool, wret, wout = (a.astype(BF16) for a in (w_branch_dn, w_branch_pool, w_branch_ret, w_out))
    poolw = pool_w.astype(BF16)
    xt = x.reshape(t, d)
    for l in range(depth):
        proj, small = _proj(xt, row3(mix_pre_norm), w_main, w_small, cos, sin, dn_conv,
                            layer=l, seq=seq, tm=1024)
        xt = _mix(xt, proj, small, alog, dtb, row3(dn_out_norm), row3(ret_out_norm), poolw,
                  row3(pool_scale), wdn, wpool, wret, wout, row3(mix_post_norm),
                  layer=l, batch=batch, seq=seq, group=16, merge_rows=8)
        xt = _ffn(xt, row3(ffn_pre_norm), ffn_gate, ffn_up, ffn_down, row3(ffn_post_norm),
                  layer=l, tm=1024, tf=256, norm_rows=128)
    return xt.reshape(batch, seq, d)
```
